```python
import jax, jax.numpy as jnp
from jax import lax
import numpy as np

D_MODEL = 1024
BATCH = 4
SEQ = 4096
DEPTH = 2
DEC_BATCH = 32
DEC_SEQ = 8
PAST_LEN = 8192
PAGE_SIZE = 128

N_MIXERS = 2
N_SB_LAYERS = (DEPTH + 1) // 2
N_RW_LAYERS = DEPTH // 2
SB_HEAD_DIM = 64
SB_HEADS = D_MODEL // SB_HEAD_DIM
Q_BLOCK = 128
RW_HEAD_DIM = 64
RW_HEADS = D_MODEL // RW_HEAD_DIM
DECAY_LORA = 64
AAA_LORA = 64
GATE_LORA = 160
D_FF = 4 * D_MODEL
NORM_EPS = 1e-6
GN_EPS = 1e-5 * RW_HEAD_DIM

kernel_name = 'stickbreak_rwkv7_hybrid_step'


def rmsnorm(x, g):
    x32 = x.astype(jnp.float32)
    y = x32 * lax.rsqrt(jnp.mean(x32 * x32, axis=-1, keepdims=True) + NORM_EPS)
    return (y * g.astype(jnp.float32)).astype(x.dtype)


def ada_norm(x, c, g, w_ada, b_ada):
    mod = jax.nn.silu(c) @ w_ada + b_ada
    shift, scale, gate = jnp.split(mod[:, None, :], 3, axis=-1)
    return rmsnorm(x, g) * (1 + scale) + shift, gate


def sqrelu_mlp(h, w_up, w_down):
    u = jax.nn.relu(h @ w_up)
    return (u * u) @ w_down


def sb_attend(q, k, v, q_pos, k_pos, bias):
    z = (jnp.einsum('bqhd,bkhd->bhqk', q, k).astype(jnp.float32) * (SB_HEAD_DIM ** -0.5)
         + bias.astype(jnp.float32)[None, :, None, None])
    mask = k_pos[None, :] < q_pos[:, None]
    log_beta = jax.nn.log_sigmoid(z)
    log_1mb = jnp.where(mask, log_beta - z, 0.0)
    tail = lax.cumsum(log_1mb, axis=3, reverse=True) - log_1mb
    a = jnp.where(mask, jnp.exp(log_beta + tail), 0.0)
    return jnp.einsum('bhqk,bkhd->bqhd', a.astype(v.dtype), v)


def sb_qkv(h, w_qkv):
    B, T, _ = h.shape
    qkv = (h @ w_qkv).reshape(B, T, 3, SB_HEADS, SB_HEAD_DIM)
    return qkv[:, :, 0], qkv[:, :, 1], qkv[:, :, 2]


def sb_mixer_prompt(h, w_qkv, w_o, bias):
    B, S, D = h.shape
    q, k, v = sb_qkv(h, w_qkv)
    nb = S // Q_BLOCK
    qb = jnp.moveaxis(q.reshape(B, nb, Q_BLOCK, SB_HEADS, SB_HEAD_DIM), 1, 0)
    k_pos = jnp.arange(S)

    def one_block(args):
        bi, q_blk = args
        q_pos = bi * Q_BLOCK + jnp.arange(Q_BLOCK)
        return sb_attend(q_blk, k, v, q_pos, k_pos, bias)

    o = lax.map(one_block, (jnp.arange(nb), qb))
    o = jnp.moveaxis(o, 0, 1).reshape(B, S, D)
    return o @ w_o, k, v


def sb_mixer_sample(h, cache_k, cache_v, layer, page_table, w_qkv, w_o, bias):
    B, T, D = h.shape
    q, k, v = sb_qkv(h, w_qkv)
    past_k = cache_k[layer][page_table]
    past_v = cache_v[layer][page_table]
    past_len = past_k.shape[1] * past_k.shape[2]
    past_k = past_k.reshape(B, past_len, SB_HEADS, SB_HEAD_DIM)
    past_v = past_v.reshape(B, past_len, SB_HEADS, SB_HEAD_DIM)
    k_all = jnp.concatenate([past_k, k.astype(past_k.dtype)], axis=1)
    v_all = jnp.concatenate([past_v, v.astype(past_v.dtype)], axis=1)
    q_pos = past_len + jnp.arange(T)
    k_pos = jnp.arange(past_len + T)
    o = sb_attend(q, k_all.astype(q.dtype), v_all.astype(q.dtype), q_pos, k_pos,
                  bias).reshape(B, T, D)
    return o @ w_o, k, v


def rwkv7_mixer(h, h_last, S0, mu, w_r, w_k, w_v, w_o, w0, w1, w2, a0, a1, a2,
                g1, g2, k_k, k_a, r_k, ln_g, ln_b):
    B, T, D = h.shape
    prev = jnp.concatenate([h_last[:, None, :].astype(h.dtype), h[:, :-1]], axis=1)
    xx = prev - h
    xr, xw, xk, xv, xa, xg = (h + xx * mu[n] for n in range(6))
    r = xr @ w_r
    k = xk @ w_k
    v = xv @ w_v
    w = -jax.nn.softplus(-(w0 + jnp.tanh(xw @ w1) @ w2)) - 0.5
    decay = jnp.exp(-jnp.exp(w.astype(jnp.float32)))
    a = jax.nn.sigmoid(a0 + (xa @ a1) @ a2)
    g = jax.nn.sigmoid(xg @ g1) @ g2

    def heads(t):
        return t.reshape(B, T, RW_HEADS, RW_HEAD_DIM).astype(jnp.float32)

    kk = heads(k * k_k)
    kk = kk / jnp.maximum(jnp.sqrt(jnp.sum(kk * kk, axis=-1, keepdims=True)), 1e-12)
    k_h = heads(k * (1 + (a - 1) * k_a))
    r_h, v_h, a_h, w_h = heads(r), heads(v), heads(a), heads(decay)

    def step(S, inp):
        r_t, w_t, k_t, v_t, kk_t, b_t = inp
        sa = jnp.einsum('bhij,bhj->bhi', S, kk_t)
        S = (S * w_t[:, :, None, :] - sa[..., None] * b_t[:, :, None, :]
             + v_t[..., None] * k_t[:, :, None, :])
        return S, jnp.einsum('bhij,bhj->bhi', S, r_t)

    xs = tuple(jnp.moveaxis(t, 1, 0) for t in (r_h, w_h, k_h, v_h, kk, kk * a_h))
    S_T, y = lax.scan(step, S0.astype(jnp.float32), xs)
    y = jnp.moveaxis(y, 0, 1)
    mean = jnp.mean(y, axis=-1, keepdims=True)
    var = jnp.mean(jnp.square(y - mean), axis=-1, keepdims=True)
    y = ((y - mean) * lax.rsqrt(var + GN_EPS)).reshape(B, T, D) * ln_g + ln_b
    bonus = jnp.sum(r_h * k_h * r_k, axis=-1, keepdims=True) * v_h
    y = (y + bonus.reshape(B, T, D)).astype(h.dtype)
    return (y * g) @ w_o, S_T.astype(S0.dtype), h[:, -1]


def setup_inputs(seed: int = 0) -> dict:
    key = jax.random.key(seed)
    ks = iter(jax.random.split(key, 48))
    f32 = jnp.float32
    D = D_MODEL

    def nrm(shape, scale=1.0):
        return scale * jax.random.normal(next(ks), shape, f32)

    def gain(shape):
        return 1.0 + 0.05 * jax.random.normal(next(ks), shape, f32)

    n_pages = PAST_LEN // PAGE_SIZE
    n_used = DEC_BATCH * n_pages
    n_phys = n_used + max(n_used // 4, 1)
    x_prompt = nrm((BATCH, SEQ, D))
    x_sample = nrm((DEC_BATCH, DEC_SEQ, D))
    cache_k = nrm((N_SB_LAYERS, n_phys, PAGE_SIZE, SB_HEADS, SB_HEAD_DIM))
    cache_v = nrm((N_SB_LAYERS, n_phys, PAGE_SIZE, SB_HEADS, SB_HEAD_DIM))
    state_wkv = nrm((N_RW_LAYERS, DEC_BATCH, RW_HEADS, RW_HEAD_DIM, RW_HEAD_DIM), 0.3)
    state_shift = nrm((N_RW_LAYERS, DEC_BATCH, D))
    page_table = jax.random.permutation(next(ks), n_phys)[:n_used].reshape(
        DEC_BATCH, n_pages).astype(jnp.int32)
    c_prompt = nrm((BATCH, D))
    c_sample = nrm((DEC_BATCH, D))
    sb_bias = (-jnp.linspace(3.0, 9.0, SB_HEADS, dtype=f32)[None, :]
               + nrm((N_SB_LAYERS, SB_HEADS), 0.1))
    return {
        'x_prompt': x_prompt, 'x_sample': x_sample,
        'cache_k': cache_k, 'cache_v': cache_v,
        'state_wkv': state_wkv, 'state_shift': state_shift,
        'page_table': page_table,
        'c_prompt': c_prompt, 'c_sample': c_sample,
        'norm_mix': gain((DEPTH, D)),
        'ada_w_mix': nrm((DEPTH, D, 3 * D), D ** -0.5),
        'ada_b_mix': nrm((DEPTH, 3 * D), 0.02),
        'norm_ffn': gain((DEPTH, D)),
        'ada_w_ffn': nrm((DEPTH, D, 3 * D), D ** -0.5),
        'ada_b_ffn': nrm((DEPTH, 3 * D), 0.02),
        'w_up': nrm((DEPTH, D, D_FF), D ** -0.5),
        'w_down': nrm((DEPTH, D_FF, D), D_FF ** -0.5),
        'sb_w_qkv': nrm((N_SB_LAYERS, D, 3 * D), D ** -0.5),
        'sb_w_o': nrm((N_SB_LAYERS, D, D), D ** -0.5),
        'sb_bias': sb_bias,
        'rw_mu': jax.random.uniform(next(ks), (N_RW_LAYERS, 6, D), f32),
        'rw_w_r': nrm((N_RW_LAYERS, D, D), D ** -0.5),
        'rw_w_k': nrm((N_RW_LAYERS, D, D), D ** -0.5),
        'rw_w_v': nrm((N_RW_LAYERS, D, D), D ** -0.5),
        'rw_w_o': nrm((N_RW_LAYERS, D, D), D ** -0.5),
        'rw_w0': nrm((N_RW_LAYERS, D), 0.5),
        'rw_w1': nrm((N_RW_LAYERS, D, DECAY_LORA), D ** -0.5),
        'rw_w2': nrm((N_RW_LAYERS, DECAY_LORA, D), DECAY_LORA ** -0.5),
        'rw_a0': nrm((N_RW_LAYERS, D), 0.1),
        'rw_a1': nrm((N_RW_LAYERS, D, AAA_LORA), D ** -0.5),
        'rw_a2': nrm((N_RW_LAYERS, AAA_LORA, D), AAA_LORA ** -0.5),
        'rw_g1': nrm((N_RW_LAYERS, D, GATE_LORA), D ** -0.5),
        'rw_g2': nrm((N_RW_LAYERS, GATE_LORA, D), GATE_LORA ** -0.5),
        'rw_k_k': 0.85 + nrm((N_RW_LAYERS, D), 0.05),
        'rw_k_a': 1.0 + nrm((N_RW_LAYERS, D), 0.05),
        'rw_r_k': nrm((N_RW_LAYERS, RW_HEADS, RW_HEAD_DIM), 0.1),
        'rw_ln_g': gain((N_RW_LAYERS, D)),
        'rw_ln_b': nrm((N_RW_LAYERS, D), 0.02),
        'final_norm': gain((D,)),
    }


def reference(x_prompt, x_sample, cache_k, cache_v, state_wkv, state_shift, page_table,
              c_prompt, c_sample, norm_mix, ada_w_mix, ada_b_mix, norm_ffn, ada_w_ffn,
              ada_b_ffn, w_up, w_down, sb_w_qkv, sb_w_o, sb_bias, rw_mu, rw_w_r, rw_w_k,
              rw_w_v, rw_w_o, rw_w0, rw_w1, rw_w2, rw_a0, rw_a1, rw_a2, rw_g1, rw_g2,
              rw_k_k, rw_k_a, rw_r_k, rw_ln_g, rw_ln_b, final_norm):
    xp, xs = x_prompt, x_sample
    kp_l, vp_l, ks_l, vs_l = [], [], [], []
    wp_l, sp_l, ws_l, ss_l = [], [], [], []
    for i in range(DEPTH):
        j = i // N_MIXERS
        hp, gp = ada_norm(xp, c_prompt, norm_mix[i], ada_w_mix[i], ada_b_mix[i])
        hs, gs = ada_norm(xs, c_sample, norm_mix[i], ada_w_mix[i], ada_b_mix[i])
        if i % N_MIXERS == 0:
            op, kp, vp = sb_mixer_prompt(hp, sb_w_qkv[j], sb_w_o[j], sb_bias[j])
            os_, ks, vs = sb_mixer_sample(hs, cache_k, cache_v, j, page_table,
                                          sb_w_qkv[j], sb_w_o[j], sb_bias[j])
            kp_l.append(kp); vp_l.append(vp); ks_l.append(ks); vs_l.append(vs)
        else:
            rw = (rw_mu[j], rw_w_r[j], rw_w_k[j], rw_w_v[j], rw_w_o[j], rw_w0[j], rw_w1[j],
                  rw_w2[j], rw_a0[j], rw_a1[j], rw_a2[j], rw_g1[j], rw_g2[j], rw_k_k[j],
                  rw_k_a[j], rw_r_k[j], rw_ln_g[j], rw_ln_b[j])
            zero_shift = jnp.zeros_like(hp[:, 0])
            zero_state = jnp.zeros((hp.shape[0],) + state_wkv.shape[2:], state_wkv.dtype)
            op, wp, sp = rwkv7_mixer(hp, zero_shift, zero_state, *rw)
            os_, ws, ss = rwkv7_mixer(hs, state_shift[j], state_wkv[j], *rw)
            wp_l.append(wp); sp_l.append(sp); ws_l.append(ws); ss_l.append(ss)
        xp = xp + gp * op
        xs = xs + gs * os_
        hp, gp = ada_norm(xp, c_prompt, norm_ffn[i], ada_w_ffn[i], ada_b_ffn[i])
        hs, gs = ada_norm(xs, c_sample, norm_ffn[i], ada_w_ffn[i], ada_b_ffn[i])
        xp = xp + gp * sqrelu_mlp(hp, w_up[i], w_down[i])
        xs = xs + gs * sqrelu_mlp(hs, w_up[i], w_down[i])
    y_prompt = rmsnorm(xp, final_norm)
    y_sample = rmsnorm(xs, final_norm)
    return (y_prompt, y_sample, jnp.stack(kp_l), jnp.stack(vp_l), jnp.stack(ks_l),
            jnp.stack(vs_l), jnp.stack(wp_l), jnp.stack(sp_l), jnp.stack(ws_l),
            jnp.stack(ss_l))
```

```python
import functools

import jax
import jax.numpy as jnp
from jax import lax
from jax.experimental import pallas as pl
from jax.experimental.pallas import tpu as pltpu

F32 = jnp.float32
BF16 = jnp.bfloat16

D = 1024
HEADS = 16
HD = 64
PAIRS = HEADS // 2
LANES = 128
DFF = 4 * D
NORM_EPS = 1e-6
GN_EPS = 1e-5 * HD
SCALE = HD ** -0.5
GATE_LORA_PAD = 256
GROUP = 8
VMEM_LIMIT = 56 * 1024 * 1024

NT = (((1,), (1,)), ((), ()))


def _params(sem):
    return pltpu.CompilerParams(dimension_semantics=sem, vmem_limit_bytes=VMEM_LIMIT)


def _dot(a, b):
    return jnp.dot(a, b, preferred_element_type=F32)


def _dot_nt(a, b):
    return lax.dot_general(a, b, NT, preferred_element_type=F32)


def _softplus(x):
    return jnp.maximum(x, 0.0) + jnp.log(1.0 + jnp.exp(-jnp.abs(x)))


def _adanorm(x, g, scale, shift):
    y = x * lax.rsqrt(jnp.mean(x * x, axis=-1, keepdims=True) + NORM_EPS)
    return (y * g) * (1.0 + scale) + shift


def _mod_spec(mod, tiles_per_seq):
    rows = mod.shape[1]
    return pl.BlockSpec((1, rows, D), lambda i: (i // tiles_per_seq, 0, 0))


def _full_spec(a):
    nd = a.ndim
    return pl.BlockSpec(a.shape, lambda i: (0,) * nd, pipeline_mode=pl.Buffered(1))


def _ada_body(c_ref, w_ref, b_ref, o_ref):
    c = c_ref[...]
    s = (c * jax.nn.sigmoid(c)).astype(BF16)
    o_ref[0] = _dot(s, w_ref[0].astype(BF16)) + b_ref[0]


def _ada_mod(c_all, w, b):
    layers, rows, tn = w.shape[0], c_all.shape[0], 768
    return pl.pallas_call(
        _ada_body,
        grid=(layers, 3 * D // tn),
        in_specs=[pl.BlockSpec((rows, D), lambda l, j: (0, 0)),
                  pl.BlockSpec((1, D, tn), lambda l, j: (l, 0, j)),
                  pl.BlockSpec((1, 1, tn), lambda l, j: (l, 0, j))],
        out_specs=pl.BlockSpec((1, rows, tn), lambda l, j: (l, 0, j)),
        out_shape=jax.ShapeDtypeStruct((layers, rows, 3 * D), F32),
        compiler_params=_params(("arbitrary", "arbitrary")),
        name="ada_mod",
    )(c_all, w, b.reshape(layers, 1, 3 * D))


def _qkv_body(x_ref, g_ref, sc_ref, sh_ref, w_ref, q_ref, k_ref, v_ref, kb_ref, vb_ref):
    h = _adanorm(x_ref[...], g_ref[...], sc_ref[0], sh_ref[0]).astype(BF16)
    q_ref[...] = _dot(h, w_ref[:, 0:D]) * SCALE
    k = _dot(h, w_ref[:, D:2 * D])
    k_ref[...] = k
    kb_ref[...] = k.astype(BF16)
    v = _dot(h, w_ref[:, 2 * D:3 * D])
    v_ref[...] = v
    vb_ref[...] = v.astype(BF16)


def _qkv(x, g, scale, shift, w, tm, tiles_per_seq):
    n = x.shape[0]
    row = pl.BlockSpec((tm, D), lambda i: (i, 0))
    return pl.pallas_call(
        _qkv_body,
        grid=(n // tm,),
        in_specs=[row, _full_spec(g), _mod_spec(scale, tiles_per_seq),
                  _mod_spec(shift, tiles_per_seq), _full_spec(w)],
        out_specs=[row] * 5,
        out_shape=[jax.ShapeDtypeStruct((n, D), F32)] * 3
        + [jax.ShapeDtypeStruct((n, D), BF16)] * 2,
        compiler_params=_params(("arbitrary",)),
        name="qkv",
    )(x, g, scale, shift, w)


def _proj_res_body(o_ref, x_ref, gate_ref, w_ref, y_ref):
    y_ref[...] = x_ref[...] + gate_ref[0] * _dot(o_ref[...].astype(BF16), w_ref[...])


def _proj_res(o, x, gate, w, tm, tiles_per_seq):
    n = x.shape[0]
    row = pl.BlockSpec((tm, D), lambda i: (i, 0))
    return pl.pallas_call(
        _proj_res_body,
        grid=(n // tm,),
        in_specs=[row, row, _mod_spec(gate, tiles_per_seq), _full_spec(w)],
        out_specs=row,
        out_shape=jax.ShapeDtypeStruct((n, D), F32),
        compiler_params=_params(("arbitrary",)),
        name="proj_res",
    )(o, x, gate, w)


def _mlp_body(x_ref, g_ref, sc_ref, sh_ref, gate_ref, wu_ref, wd_ref, fin_ref, y_ref,
              *, final_norm):
    x = x_ref[...]
    h = _adanorm(x, g_ref[...], sc_ref[0], sh_ref[0]).astype(BF16)
    acc = jnp.zeros_like(x)
    for c in range(DFF // D):
        u = jnp.maximum(_dot(h, wu_ref[:, c * D:(c + 1) * D]), 0.0)
        acc = acc + _dot((u * u).astype(BF16), wd_ref[c * D:(c + 1) * D, :])
    y = x + gate_ref[0] * acc
    if final_norm:
        y = (y * lax.rsqrt(jnp.mean(y * y, axis=-1, keepdims=True) + NORM_EPS)) * fin_ref[...]
    y_ref[...] = y


def _mlp(x, g, scale, shift, gate, wu, wd, fin, tm, tiles_per_seq, final_norm):
    n = x.shape[0]
    row = pl.BlockSpec((tm, D), lambda i: (i, 0))
    return pl.pallas_call(
        functools.partial(_mlp_body, final_norm=final_norm),
        grid=(n // tm,),
        in_specs=[row, _full_spec(g), _mod_spec(scale, tiles_per_seq),
                  _mod_spec(shift, tiles_per_seq), _mod_spec(gate, tiles_per_seq),
                  _full_spec(wu), _full_spec(wd), _full_spec(fin)],
        out_specs=row,
        out_shape=jax.ShapeDtypeStruct((n, D), F32),
        compiler_params=_params(("arbitrary",)),
        name="mlp",
    )(x, g, scale, shift, gate, wu, wd, fin)


def _norm_body(x_ref, g_ref, sc_ref, sh_ref, h_ref):
    h_ref[...] = _adanorm(x_ref[...], g_ref[...], sc_ref[0], sh_ref[0])


def _norm(x, g, scale, shift, tm, tiles_per_seq):
    n = x.shape[0]
    row = pl.BlockSpec((tm, D), lambda i: (i, 0))
    return pl.pallas_call(
        _norm_body,
        grid=(n // tm,),
        in_specs=[row, _full_spec(g), _mod_spec(scale, tiles_per_seq),
                  _mod_spec(shift, tiles_per_seq)],
        out_specs=row,
        out_shape=jax.ShapeDtypeStruct((n, D), F32),
        compiler_params=_params(("arbitrary",)),
        name="ada_norm",
    )(x, g, scale, shift)


def _suffix_ones():
    j = lax.broadcasted_iota(jnp.int32, (LANES, 2 * LANES), 0)
    s = lax.broadcasted_iota(jnp.int32, (LANES, 2 * LANES), 1)
    return jnp.where((j >= s) | (s >= LANES), 1.0, 0.0).astype(BF16)


def _sb_block(z, mask, uo, carry):
    l = -_softplus(z)
    if mask is not None:
        l = jnp.where(mask, l, 0.0)
    l_hi = l.astype(BF16)
    l_lo = (l - l_hi.astype(F32)).astype(BF16)
    cs = _dot(l_hi, uo) + _dot(l_lo, uo)
    a = jnp.exp(z + cs[:, :LANES] + carry)
    if mask is not None:
        a = jnp.where(mask, a, 0.0)
    return a, cs[:, LANES:]


def _sbp_body(bias_ref, q_ref, k_ref, v_ref, uo_ref, o_ref, acc_ref, car_ref, *, tq):
    p = pl.program_id(1)
    i = pl.program_id(2)
    nsub = tq // LANES
    lane = lax.broadcasted_iota(jnp.int32, (tq, LANES), 1)
    q = q_ref[...]
    qh = (jnp.where(lane < HD, q, 0.0).astype(BF16), jnp.where(lane >= HD, q, 0.0).astype(BF16))
    acc_ref[...] = jnp.zeros_like(acc_ref)
    car_ref[...] = jnp.zeros_like(car_ref)
    uo = uo_ref[...]

    def step(kb, masked):
        off = pl.multiple_of(kb * LANES, LANES)
        k = k_ref[pl.ds(off, LANES), :]
        v = v_ref[pl.ds(off, LANES), :]
        mask = None
        if masked:
            q_pos = i * tq + lax.broadcasted_iota(jnp.int32, (tq, LANES), 0)
            k_pos = kb * LANES + lax.broadcasted_iota(jnp.int32, (tq, LANES), 1)
            mask = k_pos < q_pos
        for hh in range(2):
            z = _dot_nt(qh[hh], k) + bias_ref[2 * p + hh]
            a, inc = _sb_block(z, mask, uo, car_ref[hh])
            acc_ref[hh] += _dot(a.astype(BF16), v)
            car_ref[hh] += inc

    for d in range(nsub):
        step(i * nsub + (nsub - 1 - d), True)

    def body(n, c):
        step(i * nsub - 1 - n, False)
        return c

    lax.fori_loop(0, i * nsub, body, 0)
    o_ref[...] = jnp.where(lane < HD, acc_ref[0], acc_ref[1]).astype(o_ref.dtype)


def _sb_prompt(q, kb, vb, bias, batch, seq, tq):
    n = q.shape[0]
    nq = seq // tq
    return pl.pallas_call(
        functools.partial(_sbp_body, tq=tq),
        grid=(batch, PAIRS, nq),
        in_specs=[pl.BlockSpec(memory_space=pltpu.SMEM),
                  pl.BlockSpec((tq, LANES), lambda b, p, i: (b * nq + i, p)),
                  pl.BlockSpec((seq, LANES), lambda b, p, i: (b, p)),
                  pl.BlockSpec((seq, LANES), lambda b, p, i: (b, p)),
                  pl.BlockSpec((LANES, 2 * LANES), lambda b, p, i: (0, 0))],
        out_specs=pl.BlockSpec((tq, LANES), lambda b, p, i: (b * nq + i, p)),
        out_shape=jax.ShapeDtypeStruct((n, D), BF16),
        scratch_shapes=[pltpu.VMEM((2, tq, LANES), F32), pltpu.VMEM((2, tq, LANES), F32)],
        compiler_params=_params(("arbitrary", "arbitrary", "arbitrary")),
        name="sb_prompt",
    )(bias, q, kb, vb, _suffix_ones())


def _sbs_body(pt_ref, q_ref, kn_ref, vn_ref, kc_ref, vc_ref, bias_ref, uo_ref, o_ref,
              qb_ref, acc_ref, car_ref, *, n_new, n_pages):
    del pt_ref
    p = pl.program_id(1)
    rows = HEADS * n_new
    row = lax.broadcasted_iota(jnp.int32, (rows, D), 0)
    col = lax.broadcasted_iota(jnp.int32, (rows, D), 1)
    own = (row // n_new) == (col // HD)
    uo = uo_ref[...]

    def block(k, v, mask):
        z = _dot_nt(qb_ref[...], k) + bias_ref[...]
        a, inc = _sb_block(z, mask, uo, car_ref[...])
        acc_ref[...] += _dot(a.astype(BF16), v)
        car_ref[...] += inc

    @pl.when(p == 0)
    def _():
        q = q_ref[...]
        qt = jnp.broadcast_to(q[None], (HEADS, n_new, D)).reshape(rows, D)
        qb_ref[...] = jnp.where(own, qt, 0.0).astype(BF16)
        acc_ref[...] = jnp.zeros_like(acc_ref)
        car_ref[...] = jnp.zeros_like(car_ref)
        pad = jnp.zeros((LANES - n_new, D), F32)
        kn = jnp.concatenate([kn_ref[...], pad], axis=0).astype(BF16)
        vn = jnp.concatenate([vn_ref[...], pad], axis=0).astype(BF16)
        r = lax.broadcasted_iota(jnp.int32, (rows, LANES), 0)
        s = lax.broadcasted_iota(jnp.int32, (rows, LANES), 1)
        block(kn, vn, s < (r % n_new))

    block(kc_ref[...].astype(BF16), vc_ref[...].astype(BF16), None)

    @pl.when(p == n_pages - 1)
    def _():
        o = jnp.where(own, acc_ref[...], 0.0).reshape(HEADS, n_new, D)
        o_ref[...] = jnp.sum(o, axis=0).astype(o_ref.dtype)


def _sb_sample(q, k_new, v_new, cache_k, cache_v, page_table, bias, n_new):
    n = q.shape[0]
    batch, n_pages = page_table.shape
    page = cache_k.shape[1]
    assert page == LANES and HEADS * n_new == LANES
    bias_rows = jnp.broadcast_to(jnp.repeat(bias, n_new)[:, None], (LANES, LANES))
    new = pl.BlockSpec((n_new, D), lambda b, p, pt: (b, 0))
    cache = pl.BlockSpec((None, page, D), lambda b, p, pt: (pt[b, n_pages - 1 - p], 0, 0))
    return pl.pallas_call(
        functools.partial(_sbs_body, n_new=n_new, n_pages=n_pages),
        grid_spec=pltpu.PrefetchScalarGridSpec(
            num_scalar_prefetch=1,
            grid=(batch, n_pages),
            in_specs=[new, new, new, cache, cache,
                      pl.BlockSpec((LANES, LANES), lambda b, p, pt: (0, 0)),
                      pl.BlockSpec((LANES, 2 * LANES), lambda b, p, pt: (0, 0))],
            out_specs=new,
            scratch_shapes=[pltpu.VMEM((LANES, D), BF16), pltpu.VMEM((LANES, D), F32),
                            pltpu.VMEM((LANES, LANES), F32)]),
        out_shape=jax.ShapeDtypeStruct((n, D), F32),
        compiler_params=_params(("arbitrary", "arbitrary")),
        name="sb_sample",
    )(page_table, q, k_new, v_new, cache_k, cache_v, bias_rows, _suffix_ones())


def _lanes(col, tm):
    return jnp.concatenate([col] * (tm // LANES), axis=1)


def _rw_proj_body(h_ref, prev_ref, mu_ref, wr_ref, wk_ref, wv_ref, w1_ref, w2_ref, a1_ref,
                  a2_ref, g1_ref, g2_ref, w0_ref, a0_ref, kk_ref, ka_ref,
                  r_out, d_out, k_out, kkn_out, b_out, v_out, g_out, *, tm):
    h = h_ref[...]
    xx = prev_ref[...] - h
    xr, xw, xk, xv, xa, xg = ((h + xx * mu_ref[n:n + 1, :]).astype(BF16) for n in range(6))
    r_out[...] = _dot_nt(wr_ref[...], xr)
    k = _dot_nt(wk_ref[...], xk)
    v_out[...] = _dot_nt(wv_ref[...], xv)
    w_lin = _lanes(w0_ref[...], tm) + _dot_nt(w2_ref[...], jnp.tanh(_dot(xw, w1_ref[...])).astype(BF16))
    w = -_softplus(-w_lin) - 0.5
    d_out[...] = jnp.exp(-jnp.exp(w))
    a = jax.nn.sigmoid(_lanes(a0_ref[...], tm) + _dot_nt(a2_ref[...], _dot(xa, a1_ref[...]).astype(BF16)))
    g_out[...] = _dot_nt(g2_ref[...], jax.nn.sigmoid(_dot(xg, g1_ref[...])).astype(BF16))
    kk = (k * _lanes(kk_ref[...], tm)).reshape(HEADS, HD, tm)
    norm = jnp.sqrt(jnp.sum(kk * kk, axis=1, keepdims=True))
    kk = (kk / jnp.maximum(norm, 1e-12)).reshape(D, tm)
    kkn_out[...] = kk
    b_out[...] = kk * a
    k_out[...] = k * (1.0 + (a - 1.0) * _lanes(ka_ref[...], tm))


def _rw_proj(h, prev, mats, cols, tm):
    n = h.shape[0]
    row = pl.BlockSpec((tm, D), lambda i: (i, 0))
    out = pl.BlockSpec((D, tm), lambda i: (0, i))
    return pl.pallas_call(
        functools.partial(_rw_proj_body, tm=tm),
        grid=(n // tm,),
        in_specs=[row, row] + [_full_spec(m) for m in mats] + [_full_spec(c) for c in cols],
        out_specs=[out] * 7,
        out_shape=[jax.ShapeDtypeStruct((D, n), F32)] * 7,
        compiler_params=_params(("arbitrary",)),
        name="rw_proj",
    )(h, prev, *mats, *cols)


def _scan_body(r_ref, d_ref, k_ref, kk_ref, b_ref, v_ref, s0_ref, y_ref, sT_ref,
               st_ref, vbuf_ref, ybuf_ref, *, n_groups, per_group_state, chunk_axis):
    first = lax.broadcasted_iota(jnp.int32, (HD, LANES), 1) < HD
    vbuf_ref[...] = v_ref[...].T

    if not per_group_state:
        c = pl.program_id(chunk_axis)

        @pl.when(c == 0)
        def _():
            st_ref[...] = s0_ref[...].T

    def group(g, carry):
        if per_group_state:
            st_ref[...] = s0_ref[g].T
        base = pl.multiple_of((g // (LANES // GROUP)) * LANES, LANES)
        gl = g % (LANES // GROUP)
        shift = jnp.where(gl == 0, 0, LANES - gl * GROUP)
        tiles = [pltpu.roll(ref[:, pl.ds(base, LANES)], shift, 1)
                 for ref in (d_ref, kk_ref, b_ref, k_ref, r_ref)]
        s_t = st_ref[...]
        for s in range(GROUP):
            dc, kkc, bc, kc, rc = (jnp.where(first, x[0:HD, s:s + 1], x[HD:2 * HD, s:s + 1])
                                   for x in tiles)
            t = g * GROUP + s
            sa = jnp.sum(s_t * kkc, axis=0, keepdims=True)
            s_t = s_t * dc - bc * sa + kc * vbuf_ref[pl.ds(t, 1), :]
            ybuf_ref[pl.ds(t, 1), :] = jnp.sum(s_t * rc, axis=0, keepdims=True)
        st_ref[...] = s_t
        if per_group_state:
            sT_ref[g] = s_t.T
        return carry

    lax.fori_loop(0, n_groups, group, 0)
    y_ref[...] = ybuf_ref[...].T

    if not per_group_state:
        @pl.when(c == pl.num_programs(chunk_axis) - 1)
        def _():
            sT_ref[...] = st_ref[...].T


def _rw_scan_prompt(r, d, k, kk, b, v, batch, seq):
    nc = seq // LANES
    tile = pl.BlockSpec((LANES, LANES), lambda p, bb, c: (p, bb * nc + c))
    st = pl.BlockSpec((None, None, LANES, HD), lambda p, bb, c: (bb, p, 0, 0))
    s0 = jnp.zeros((batch, PAIRS, LANES, HD), F32)
    y, s_t = pl.pallas_call(
        functools.partial(_scan_body, n_groups=LANES // GROUP, per_group_state=False, chunk_axis=2),
        grid=(PAIRS, batch, nc),
        in_specs=[tile] * 6 + [st],
        out_specs=[tile, st],
        out_shape=[jax.ShapeDtypeStruct(r.shape, F32),
                   jax.ShapeDtypeStruct((batch, PAIRS, LANES, HD), F32)],
        scratch_shapes=[pltpu.VMEM((HD, LANES), F32), pltpu.VMEM((LANES, LANES), F32),
                        pltpu.VMEM((LANES, LANES), F32)],
        compiler_params=_params(("arbitrary", "arbitrary", "arbitrary")),
        name="rw_scan_prompt",
    )(r, d, k, kk, b, v, s0)
    return y, s_t


def _rw_scan_sample(r, d, k, kk, b, v, s0, batch, seq):
    assert seq == GROUP
    n = batch * seq
    tile = pl.BlockSpec((LANES, n), lambda p: (p, 0))
    st = pl.BlockSpec((batch, None, LANES, HD), lambda p: (0, p, 0, 0))
    y, s_t = pl.pallas_call(
        functools.partial(_scan_body, n_groups=batch, per_group_state=True, chunk_axis=None),
        grid=(PAIRS,),
        in_specs=[tile] * 6 + [st],
        out_specs=[tile, st],
        out_shape=[jax.ShapeDtypeStruct(r.shape, F32),
                   jax.ShapeDtypeStruct((batch, PAIRS, LANES, HD), F32)],
        scratch_shapes=[pltpu.VMEM((HD, LANES), F32), pltpu.VMEM((n, LANES), F32),
                        pltpu.VMEM((n, LANES), F32)],
        compiler_params=_params(("arbitrary",)),
        name="rw_scan_sample",
    )(r, d, k, kk, b, v, s0)
    return y, s_t


def _rw_out_body(y_ref, r_ref, k_ref, v_ref, g_ref, x_ref, gate_ref, wo_ref, lng_ref, lnb_ref,
                 rk_ref, o_ref, *, tm):
    y = y_ref[...].reshape(HEADS, HD, tm)
    mean = jnp.mean(y, axis=1, keepdims=True)
    var = jnp.mean(jnp.square(y - mean), axis=1, keepdims=True)
    y = ((y - mean) * lax.rsqrt(var + GN_EPS)).reshape(D, tm) * _lanes(lng_ref[...], tm) \
        + _lanes(lnb_ref[...], tm)
    rk = (r_ref[...] * k_ref[...] * _lanes(rk_ref[...], tm)).reshape(HEADS, HD, tm)
    bonus = jnp.sum(rk, axis=1, keepdims=True) * v_ref[...].reshape(HEADS, HD, tm)
    yg = (y + bonus.reshape(D, tm)) * g_ref[...]
    o = _dot(yg.T.astype(BF16), wo_ref[...])
    o_ref[...] = x_ref[...] + gate_ref[0] * o


def _rw_out(y, r, k, v, g, x, gate, wo, cols, tm, tiles_per_seq):
    n = x.shape[0]
    row = pl.BlockSpec((tm, D), lambda i: (i, 0))
    cm = pl.BlockSpec((D, tm), lambda i: (0, i))
    return pl.pallas_call(
        functools.partial(_rw_out_body, tm=tm),
        grid=(n // tm,),
        in_specs=[cm] * 5 + [row, _mod_spec(gate, tiles_per_seq), _full_spec(wo)]
        + [_full_spec(c) for c in cols],
        out_specs=row,
        out_shape=jax.ShapeDtypeStruct((n, D), F32),
        compiler_params=_params(("arbitrary",)),
        name="rw_out",
    )(y, r, k, v, g, x, gate, wo, *cols)


def _col(p):
    return jnp.broadcast_to(p.reshape(D, 1), (D, LANES))


def kernel(x_prompt, x_sample, cache_k, cache_v, state_wkv, state_shift, page_table, c_prompt, c_sample, norm_mix, ada_w_mix, ada_b_mix, norm_ffn, ada_w_ffn, ada_b_ffn, w_up, w_down, sb_w_qkv, sb_w_o, sb_bias, rw_mu, rw_w_r, rw_w_k, rw_w_v, rw_w_o, rw_w0, rw_w1, rw_w2, rw_a0, rw_a1, rw_a2, rw_g1, rw_g2, rw_k_k, rw_k_a, rw_r_k, rw_ln_g, rw_ln_b, final_norm):
    batch, seq, _ = x_prompt.shape
    dbatch, dseq, _ = x_sample.shape
    depth = norm_mix.shape[0]
    n_p, n_s = batch * seq, dbatch * dseq
    tm_p, tm_s = 512, n_s
    tps_p, tps_s = seq // tm_p, 1

    mod_mix = _ada_mod(jnp.concatenate([c_prompt, c_sample], axis=0), ada_w_mix, ada_b_mix)
    mod_ffn = _ada_mod(jnp.concatenate([c_prompt, c_sample], axis=0), ada_w_ffn, ada_b_ffn)

    def split(mod):
        parts = [mod[:, n * D:(n + 1) * D] for n in range(3)]
        return ([m[:batch].reshape(batch, 1, D) for m in parts],
                [jnp.repeat(m[batch:], dseq, axis=0).reshape(1, n_s, D) for m in parts])

    xp = x_prompt.reshape(n_p, D)
    xs = x_sample.reshape(n_s, D)
    fin = final_norm.reshape(1, D)
    outs = {}

    for i in range(depth):
        j = i // 2
        g_mix = norm_mix[i].reshape(1, D)
        (sh_p, sc_p, ga_p), (sh_s, sc_s, ga_s) = split(mod_mix[i])
        if i % 2 == 0:
            w_qkv = sb_w_qkv[j].astype(BF16)
            w_o = sb_w_o[j].astype(BF16)
            qp, kp, vp, kbp, vbp = _qkv(xp, g_mix, sc_p, sh_p, w_qkv, tm_p, tps_p)
            qs, ks, vs, _, _ = _qkv(xs, g_mix, sc_s, sh_s, w_qkv, tm_s, tps_s)
            op = _sb_prompt(qp, kbp, vbp, sb_bias[j], batch, seq, 256)
            n_phys, page = cache_k.shape[1], cache_k.shape[2]
            os_ = _sb_sample(qs, ks, vs, cache_k[j].reshape(n_phys, page, D),
                             cache_v[j].reshape(n_phys, page, D), page_table, sb_bias[j], dseq)
            xp = _proj_res(op, xp, ga_p, w_o, tm_p, tps_p)
            xs = _proj_res(os_, xs, ga_s, w_o, tm_s, tps_s)
            outs.setdefault("kp", []).append(kp.reshape(batch, seq, HEADS, HD))
            outs.setdefault("vp", []).append(vp.reshape(batch, seq, HEADS, HD))
            outs.setdefault("ks", []).append(ks.reshape(dbatch, dseq, HEADS, HD))
            outs.setdefault("vs", []).append(vs.reshape(dbatch, dseq, HEADS, HD))
        else:
            pad = GATE_LORA_PAD - rw_g1.shape[-1]
            mats = (jnp.pad(rw_mu[j], ((0, 2), (0, 0))),
                    rw_w_r[j].T.astype(BF16), rw_w_k[j].T.astype(BF16), rw_w_v[j].T.astype(BF16),
                    rw_w1[j].astype(BF16), rw_w2[j].T.astype(BF16),
                    rw_a1[j].astype(BF16), rw_a2[j].T.astype(BF16),
                    jnp.pad(rw_g1[j], ((0, 0), (0, pad))).astype(BF16),
                    jnp.pad(rw_g2[j], ((0, pad), (0, 0))).T.astype(BF16))
            cols = (_col(rw_w0[j]), _col(rw_a0[j]), _col(rw_k_k[j]), _col(rw_k_a[j]))
            ocols = (_col(rw_ln_g[j]), _col(rw_ln_b[j]), _col(rw_r_k[j]))
            w_o = rw_w_o[j].astype(BF16)

            hp = _norm(xp, g_mix, sc_p, sh_p, tm_p, tps_p).reshape(batch, seq, D)
            hs = _norm(xs, g_mix, sc_s, sh_s, tm_s, tps_s).reshape(dbatch, dseq, D)
            prev_p = jnp.concatenate([jnp.zeros((batch, 1, D), F32), hp[:, :-1]], axis=1)
            prev_s = jnp.concatenate([state_shift[j][:, None, :], hs[:, :-1]], axis=1)

            rp, dp, kp_, kkp, bp, vp_, gp = _rw_proj(hp.reshape(n_p, D), prev_p.reshape(n_p, D),
                                                     mats, cols, 256)
            rs, ds, ks_, kks, bs, vs_, gs = _rw_proj(hs.reshape(n_s, D), prev_s.reshape(n_s, D),
                                                     mats, cols, n_s)
            yp, wkv_p = _rw_scan_prompt(rp, dp, kp_, kkp, bp, vp_, batch, seq)
            s0 = state_wkv[j].reshape(dbatch, PAIRS, LANES, HD)
            ys, wkv_s = _rw_scan_sample(rs, ds, ks_, kks, bs, vs_, s0, dbatch, dseq)
            xp = _rw_out(yp, rp, kp_, vp_, gp, xp, ga_p, w_o, ocols, 256, seq // 256)
            xs = _rw_out(ys, rs, ks_, vs_, gs, xs, ga_s, w_o, ocols, n_s, 1)
            outs.setdefault("wp", []).append(wkv_p.reshape(batch, HEADS, HD, HD))
            outs.setdefault("sp", []).append(hp[:, -1])
            outs.setdefault("ws", []).append(wkv_s.reshape(dbatch, HEADS, HD, HD))
            outs.setdefault("ss", []).append(hs[:, -1])

        g_ffn = norm_ffn[i].reshape(1, D)
        (sh_p, sc_p, ga_p), (sh_s, sc_s, ga_s) = split(mod_ffn[i])
        wu = w_up[i].astype(BF16)
        wd = w_down[i].astype(BF16)
        last = i == depth - 1
        xp = _mlp(xp, g_ffn, sc_p, sh_p, ga_p, wu, wd, fin, tm_p, tps_p, last)
        xs = _mlp(xs, g_ffn, sc_s, sh_s, ga_s, wu, wd, fin, tm_s, tps_s, last)

    return (xp.reshape(batch, seq, D), xs.reshape(dbatch, dseq, D),
            jnp.stack(outs["kp"]), jnp.stack(outs["vp"]), jnp.stack(outs["ks"]),
            jnp.stack(outs["vs"]), jnp.stack(outs["wp"]), jnp.stack(outs["sp"]),
            jnp.stack(outs["ws"]), jnp.stack(outs["ss"]))
```

```python
import functools

import jax
import jax.numpy as jnp
from jax import lax
from jax.experimental import pallas as pl
from jax.experimental.pallas import tpu as pltpu

F32 = jnp.float32
BF16 = jnp.bfloat16

D = 1024
HEADS = 16
HD = 64
PAIRS = HEADS // 2
LANES = 128
DFF = 4 * D
NORM_EPS = 1e-6
GN_EPS = 1e-5 * HD
SCALE = HD ** -0.5
GATE_LORA_PAD = 256
GROUP = 8
VMEM_LIMIT = 56 * 1024 * 1024

NT = (((1,), (1,)), ((), ()))


def _params(sem):
    return pltpu.CompilerParams(dimension_semantics=sem, vmem_limit_bytes=VMEM_LIMIT)


def _dot(a, b):
    return jnp.dot(a, b, preferred_element_type=F32)


def _dot_nt(a, b):
    return lax.dot_general(a, b, NT, preferred_element_type=F32)


def _softplus(x):
    return jnp.maximum(x, 0.0) + jnp.log(1.0 + jnp.exp(-jnp.abs(x)))


def _adanorm(x, g, scale, shift):
    y = x * lax.rsqrt(jnp.mean(x * x, axis=-1, keepdims=True) + NORM_EPS)
    return (y * g) * (1.0 + scale) + shift


def _mod_spec(mod, tiles_per_seq):
    rows = mod.shape[1]
    return pl.BlockSpec((1, rows, D), lambda i: (i // tiles_per_seq, 0, 0))


def _full_spec(a):
    nd = a.ndim
    return pl.BlockSpec(a.shape, lambda i: (0,) * nd, pipeline_mode=pl.Buffered(1))


def _ada_body(c_ref, w_ref, b_ref, o_ref):
    c = c_ref[...]
    s = (c * jax.nn.sigmoid(c)).astype(BF16)
    o_ref[0] = _dot(s, w_ref[0].astype(BF16)) + b_ref[0]


def _ada_mod(c_all, w, b):
    layers, rows, tn = w.shape[0], c_all.shape[0], 768
    return pl.pallas_call(
        _ada_body,
        grid=(layers, 3 * D // tn),
        in_specs=[pl.BlockSpec((rows, D), lambda l, j: (0, 0)),
                  pl.BlockSpec((1, D, tn), lambda l, j: (l, 0, j)),
                  pl.BlockSpec((1, 1, tn), lambda l, j: (l, 0, j))],
        out_specs=pl.BlockSpec((1, rows, tn), lambda l, j: (l, 0, j)),
        out_shape=jax.ShapeDtypeStruct((layers, rows, 3 * D), F32),
        compiler_params=_params(("arbitrary", "arbitrary")),
        name="ada_mod",
    )(c_all, w, b.reshape(layers, 1, 3 * D))


def _qkv_body(x_ref, g_ref, sc_ref, sh_ref, w_ref, q_ref, k_ref, v_ref, kb_ref, vb_ref):
    h = _adanorm(x_ref[...], g_ref[...], sc_ref[0], sh_ref[0]).astype(BF16)
    q_ref[...] = _dot(h, w_ref[:, 0:D]) * SCALE
    k = _dot(h, w_ref[:, D:2 * D])
    k_ref[...] = k
    kb_ref[...] = k.astype(BF16)
    v = _dot(h, w_ref[:, 2 * D:3 * D])
    v_ref[...] = v
    vb_ref[...] = v.astype(BF16)


def _qkv(x, g, scale, shift, w, tm, tiles_per_seq):
    n = x.shape[0]
    row = pl.BlockSpec((tm, D), lambda i: (i, 0))
    return pl.pallas_call(
        _qkv_body,
        grid=(n // tm,),
        in_specs=[row, _full_spec(g), _mod_spec(scale, tiles_per_seq),
                  _mod_spec(shift, tiles_per_seq), _full_spec(w)],
        out_specs=[row] * 5,
        out_shape=[jax.ShapeDtypeStruct((n, D), F32)] * 3
        + [jax.ShapeDtypeStruct((n, D), BF16)] * 2,
        compiler_params=_params(("arbitrary",)),
        name="qkv",
    )(x, g, scale, shift, w)


def _proj_res_body(o_ref, x_ref, gate_ref, w_ref, y_ref):
    y_ref[...] = x_ref[...] + gate_ref[0] * _dot(o_ref[...].astype(BF16), w_ref[...])


def _proj_res(o, x, gate, w, tm, tiles_per_seq):
    n = x.shape[0]
    row = pl.BlockSpec((tm, D), lambda i: (i, 0))
    return pl.pallas_call(
        _proj_res_body,
        grid=(n // tm,),
        in_specs=[row, row, _mod_spec(gate, tiles_per_seq), _full_spec(w)],
        out_specs=row,
        out_shape=jax.ShapeDtypeStruct((n, D), F32),
        compiler_params=_params(("arbitrary",)),
        name="proj_res",
    )(o, x, gate, w)


def _mlp_body(x_ref, g_ref, sc_ref, sh_ref, gate_ref, wu_ref, wd_ref, fin_ref, y_ref,
              *, final_norm):
    x = x_ref[...]
    h = _adanorm(x, g_ref[...], sc_ref[0], sh_ref[0]).astype(BF16)
    acc = jnp.zeros_like(x)
    for c in range(DFF // D):
        u = jnp.maximum(_dot(h, wu_ref[:, c * D:(c + 1) * D]), 0.0)
        acc = acc + _dot((u * u).astype(BF16), wd_ref[c * D:(c + 1) * D, :])
    y = x + gate_ref[0] * acc
    if final_norm:
        y = (y * lax.rsqrt(jnp.mean(y * y, axis=-1, keepdims=True) + NORM_EPS)) * fin_ref[...]
    y_ref[...] = y


def _mlp(x, g, scale, shift, gate, wu, wd, fin, tm, tiles_per_seq, final_norm):
    n = x.shape[0]
    row = pl.BlockSpec((tm, D), lambda i: (i, 0))
    return pl.pallas_call(
        functools.partial(_mlp_body, final_norm=final_norm),
        grid=(n // tm,),
        in_specs=[row, _full_spec(g), _mod_spec(scale, tiles_per_seq),
                  _mod_spec(shift, tiles_per_seq), _mod_spec(gate, tiles_per_seq),
                  _full_spec(wu), _full_spec(wd), _full_spec(fin)],
        out_specs=row,
        out_shape=jax.ShapeDtypeStruct((n, D), F32),
        compiler_params=_params(("arbitrary",)),
        name="mlp",
    )(x, g, scale, shift, gate, wu, wd, fin)


def _norm_body(x_ref, g_ref, sc_ref, sh_ref, h_ref):
    h_ref[...] = _adanorm(x_ref[...], g_ref[...], sc_ref[0], sh_ref[0])


def _norm(x, g, scale, shift, tm, tiles_per_seq):
    n = x.shape[0]
    row = pl.BlockSpec((tm, D), lambda i: (i, 0))
    return pl.pallas_call(
        _norm_body,
        grid=(n // tm,),
        in_specs=[row, _full_spec(g), _mod_spec(scale, tiles_per_seq),
                  _mod_spec(shift, tiles_per_seq)],
        out_specs=row,
        out_shape=jax.ShapeDtypeStruct((n, D), F32),
        compiler_params=_params(("arbitrary",)),
        name="ada_norm",
    )(x, g, scale, shift)


def _suffix_ones():
    j = lax.broadcasted_iota(jnp.int32, (2 * LANES, 2 * LANES), 0) % LANES
    s = lax.broadcasted_iota(jnp.int32, (2 * LANES, 2 * LANES), 1)
    return jnp.where((j >= s) | (s >= LANES), 1.0, 0.0).astype(BF16)


def _neg_abs(x):
    bits = lax.bitcast_convert_type(x, jnp.uint32) | jnp.uint32(0x80000000)
    return lax.bitcast_convert_type(bits, F32)


def _sb_weights(z, mask, uo, carry):
    n = z.shape[1] // LANES
    sp = jnp.maximum(z, 0.0) + jnp.log(1.0 + jnp.exp(_neg_abs(z)))
    if mask is not None:
        sp = jnp.where(mask, sp, 0.0)
    hi = sp.astype(BF16)
    lo = (sp - hi.astype(F32)).astype(BF16)
    parts = [None] * n
    for s in reversed(range(n)):
        sl = slice(s * LANES, (s + 1) * LANES)
        cs = _dot(jnp.concatenate([hi[:, sl], lo[:, sl]], axis=1), uo)
        parts[s] = jnp.exp(z[:, sl] - cs[:, :LANES] - carry)
        carry = carry + cs[:, LANES:]
    a = parts[0] if n == 1 else jnp.concatenate(parts, axis=1)
    if mask is not None:
        a = jnp.where(mask, a, 0.0)
    return a, carry


def _sbp_body(bias_ref, q_ref, k_ref, v_ref, uo_ref, o_ref, acc_ref, car_ref, *, tq):
    p = pl.program_id(1)
    i = pl.program_id(2)
    wide = 2 * tq
    lane = lax.broadcasted_iota(jnp.int32, (tq, LANES), 1)
    q = q_ref[...]
    qh = (jnp.where(lane < HD, q, 0.0).astype(BF16), jnp.where(lane >= HD, q, 0.0).astype(BF16))
    acc_ref[...] = jnp.zeros_like(acc_ref)
    car_ref[...] = jnp.zeros_like(car_ref)
    uo = uo_ref[...]

    def block(off, width, masked):
        k = k_ref[pl.ds(off, width), :]
        v = v_ref[pl.ds(off, width), :]
        mask = None
        if masked:
            mask = (lax.broadcasted_iota(jnp.int32, (tq, width), 1)
                    < lax.broadcasted_iota(jnp.int32, (tq, width), 0))
        for hh in range(2):
            z = _dot_nt(qh[hh], k) + bias_ref[2 * p + hh]
            a, car = _sb_weights(z, mask, uo, car_ref[hh])
            acc_ref[hh] += _dot(a.astype(BF16), v)
            car_ref[hh] = car

    block(pl.multiple_of(i * tq, tq), tq, True)

    @pl.when(i % 2 == 1)
    def _():
        block(pl.multiple_of((i - 1) * tq, tq), tq, False)

    def body(n, c):
        block(pl.multiple_of((i // 2 - 1 - n) * wide, wide), wide, False)
        return c

    lax.fori_loop(0, i // 2, body, 0)
    o_ref[...] = jnp.where(lane < HD, acc_ref[0], acc_ref[1]).astype(o_ref.dtype)


def _sb_prompt(q, kb, vb, bias, batch, seq, tq):
    n = q.shape[0]
    nq = seq // tq
    return pl.pallas_call(
        functools.partial(_sbp_body, tq=tq),
        grid=(batch, PAIRS, nq),
        in_specs=[pl.BlockSpec(memory_space=pltpu.SMEM),
                  pl.BlockSpec((tq, LANES), lambda b, p, i: (b * nq + i, p)),
                  pl.BlockSpec((seq, LANES), lambda b, p, i: (b, p)),
                  pl.BlockSpec((seq, LANES), lambda b, p, i: (b, p)),
                  pl.BlockSpec((2 * LANES, 2 * LANES), lambda b, p, i: (0, 0))],
        out_specs=pl.BlockSpec((tq, LANES), lambda b, p, i: (b * nq + i, p)),
        out_shape=jax.ShapeDtypeStruct((n, D), BF16),
        scratch_shapes=[pltpu.VMEM((2, tq, LANES), F32), pltpu.VMEM((2, tq, LANES), F32)],
        compiler_params=_params(("arbitrary", "arbitrary", "arbitrary")),
        name="sb_prompt",
    )(bias, q, kb, vb, _suffix_ones())


def _sbs_body(pt_ref, q_ref, kn_ref, vn_ref, kc_ref, vc_ref, bias_ref, uo_ref, o_ref,
              qh_ref, knew_ref, vnew_ref, acc_ref, car_ref, *, n_new, n_pages):
    del pt_ref
    p = pl.program_id(1)
    uo = uo_ref[...]

    def block(k_ref, v_ref, mask):
        def head(ref, h):
            return ref[pl.ds(h, LANES, stride=HEADS), :].astype(BF16)

        z = jnp.concatenate([_dot_nt(qh_ref[h].astype(BF16), head(k_ref, h))
                             for h in range(HEADS)], axis=0) + bias_ref[...]
        a, car = _sb_weights(z, mask, uo, car_ref[...])
        car_ref[...] = car
        acc_ref[...] += jnp.concatenate(
            [_dot(a[h * n_new:(h + 1) * n_new].astype(BF16), head(v_ref, h))
             for h in range(HEADS)], axis=0)

    @pl.when(p == 0)
    def _():
        acc_ref[...] = jnp.zeros_like(acc_ref)
        car_ref[...] = jnp.zeros_like(car_ref)
        knew_ref[...] = jnp.zeros_like(knew_ref)
        vnew_ref[...] = jnp.zeros_like(vnew_ref)
        for h in range(HEADS):
            qh_ref[h] = q_ref[:, h * HD:(h + 1) * HD]
            knew_ref[pl.ds(h, n_new, stride=HEADS), :] = kn_ref[:, h * HD:(h + 1) * HD]
            vnew_ref[pl.ds(h, n_new, stride=HEADS), :] = vn_ref[:, h * HD:(h + 1) * HD]
        r = lax.broadcasted_iota(jnp.int32, (LANES, LANES), 0)
        s = lax.broadcasted_iota(jnp.int32, (LANES, LANES), 1)
        block(knew_ref, vnew_ref, s < (r % n_new))

    block(kc_ref, vc_ref, None)

    @pl.when(p == n_pages - 1)
    def _():
        for h in range(HEADS):
            o_ref[:, h * HD:(h + 1) * HD] = acc_ref[h * n_new:(h + 1) * n_new, :]


def _sb_sample(q, k_new, v_new, cache_k, cache_v, page_table, bias, n_new):
    n = q.shape[0]
    batch, n_pages = page_table.shape
    page = cache_k.shape[1] // HEADS
    assert page == LANES and HEADS * n_new == LANES
    bias_rows = jnp.broadcast_to(jnp.repeat(bias, n_new)[:, None], (LANES, LANES))
    new = pl.BlockSpec((n_new, D), lambda b, p, pt: (b, 0))
    cache = pl.BlockSpec((None, page * HEADS, HD),
                         lambda b, p, pt: (pt[b, n_pages - 1 - p], 0, 0))
    return pl.pallas_call(
        functools.partial(_sbs_body, n_new=n_new, n_pages=n_pages),
        grid_spec=pltpu.PrefetchScalarGridSpec(
            num_scalar_prefetch=1,
            grid=(batch, n_pages),
            in_specs=[new, new, new, cache, cache,
                      pl.BlockSpec((LANES, LANES), lambda b, p, pt: (0, 0)),
                      pl.BlockSpec((2 * LANES, 2 * LANES), lambda b, p, pt: (0, 0))],
            out_specs=new,
            scratch_shapes=[pltpu.VMEM((HEADS, n_new, HD), F32),
                            pltpu.VMEM((page * HEADS, HD), F32), pltpu.VMEM((page * HEADS, HD), F32),
                            pltpu.VMEM((LANES, HD), F32), pltpu.VMEM((LANES, LANES), F32)]),
        out_shape=jax.ShapeDtypeStruct((n, D), F32),
        compiler_params=_params(("arbitrary", "arbitrary")),
        name="sb_sample",
    )(page_table, q, k_new, v_new, cache_k, cache_v, bias_rows, _suffix_ones())


def _lanes(col, tm):
    return jnp.concatenate([col] * (tm // LANES), axis=1)


def _rw_proj_body(h_ref, prev_ref, mu_ref, wr_ref, wk_ref, wv_ref, w1_ref, w2_ref, a1_ref,
                  a2_ref, g1_ref, g2_ref, w0_ref, a0_ref, kk_ref, ka_ref,
                  r_out, d_out, k_out, kkn_out, b_out, v_out, g_out, *, tm):
    h = h_ref[...]
    xx = prev_ref[...] - h
    xr, xw, xk, xv, xa, xg = ((h + xx * mu_ref[n:n + 1, :]).astype(BF16) for n in range(6))
    r_out[...] = _dot_nt(wr_ref[...], xr)
    k = _dot_nt(wk_ref[...], xk)
    v_out[...] = _dot_nt(wv_ref[...], xv)
    w_lin = _lanes(w0_ref[...], tm) + _dot_nt(w2_ref[...], jnp.tanh(_dot(xw, w1_ref[...])).astype(BF16))
    w = -_softplus(-w_lin) - 0.5
    d_out[...] = jnp.exp(-jnp.exp(w))
    a = jax.nn.sigmoid(_lanes(a0_ref[...], tm) + _dot_nt(a2_ref[...], _dot(xa, a1_ref[...]).astype(BF16)))
    g_out[...] = _dot_nt(g2_ref[...], jax.nn.sigmoid(_dot(xg, g1_ref[...])).astype(BF16))
    kk = (k * _lanes(kk_ref[...], tm)).reshape(HEADS, HD, tm)
    norm = jnp.sqrt(jnp.sum(kk * kk, axis=1, keepdims=True))
    kk = (kk / jnp.maximum(norm, 1e-12)).reshape(D, tm)
    kkn_out[...] = kk
    b_out[...] = kk * a
    k_out[...] = k * (1.0 + (a - 1.0) * _lanes(ka_ref[...], tm))


def _rw_proj(h, prev, mats, cols, tm):
    n = h.shape[0]
    row = pl.BlockSpec((tm, D), lambda i: (i, 0))
    out = pl.BlockSpec((D, tm), lambda i: (0, i))
    return pl.pallas_call(
        functools.partial(_rw_proj_body, tm=tm),
        grid=(n // tm,),
        in_specs=[row, row] + [_full_spec(m) for m in mats] + [_full_spec(c) for c in cols],
        out_specs=[out] * 7,
        out_shape=[jax.ShapeDtypeStruct((D, n), F32)] * 7,
        compiler_params=_params(("arbitrary",)),
        name="rw_proj",
    )(h, prev, *mats, *cols)


def _scan_body(r_ref, d_ref, k_ref, kk_ref, b_ref, v_ref, s0_ref, y_ref, sT_ref,
               st_ref, vbuf_ref, ybuf_ref, *, n_groups, per_group_state, chunk_axis):
    first = lax.broadcasted_iota(jnp.int32, (HD, LANES), 1) < HD
    vbuf_ref[...] = v_ref[...].T

    if not per_group_state:
        c = pl.program_id(chunk_axis)

        @pl.when(c == 0)
        def _():
            st_ref[...] = s0_ref[...].T

    def group(g, carry):
        if per_group_state:
            st_ref[...] = s0_ref[g].T
        base = pl.multiple_of((g // (LANES // GROUP)) * LANES, LANES)
        gl = g % (LANES // GROUP)
        shift = jnp.where(gl == 0, 0, LANES - gl * GROUP)
        tiles = [pltpu.roll(ref[:, pl.ds(base, LANES)], shift, 1)
                 for ref in (d_ref, kk_ref, b_ref, k_ref, r_ref)]
        s_t = st_ref[...]
        for s in range(GROUP):
            dc, kkc, bc, kc, rc = (jnp.where(first, x[0:HD, s:s + 1], x[HD:2 * HD, s:s + 1])
                                   for x in tiles)
            t = g * GROUP + s
            sa = jnp.sum(s_t * kkc, axis=0, keepdims=True)
            s_t = s_t * dc - bc * sa + kc * vbuf_ref[pl.ds(t, 1), :]
            ybuf_ref[pl.ds(t, 1), :] = jnp.sum(s_t * rc, axis=0, keepdims=True)
        st_ref[...] = s_t
        if per_group_state:
            sT_ref[g] = s_t.T
        return carry

    lax.fori_loop(0, n_groups, group, 0)
    y_ref[...] = ybuf_ref[...].T

    if not per_group_state:
        @pl.when(c == pl.num_programs(chunk_axis) - 1)
        def _():
            sT_ref[...] = st_ref[...].T


def _rw_scan_prompt(r, d, k, kk, b, v, batch, seq):
    nc = seq // LANES
    tile = pl.BlockSpec((LANES, LANES), lambda p, bb, c: (p, bb * nc + c))
    st = pl.BlockSpec((None, None, LANES, HD), lambda p, bb, c: (bb, p, 0, 0))
    s0 = jnp.zeros((batch, PAIRS, LANES, HD), F32)
    y, s_t = pl.pallas_call(
        functools.partial(_scan_body, n_groups=LANES // GROUP, per_group_state=False, chunk_axis=2),
        grid=(PAIRS, batch, nc),
        in_specs=[tile] * 6 + [st],
        out_specs=[tile, st],
        out_shape=[jax.ShapeDtypeStruct(r.shape, F32),
                   jax.ShapeDtypeStruct((batch, PAIRS, LANES, HD), F32)],
        scratch_shapes=[pltpu.VMEM((HD, LANES), F32), pltpu.VMEM((LANES, LANES), F32),
                        pltpu.VMEM((LANES, LANES), F32)],
        compiler_params=_params(("arbitrary", "arbitrary", "arbitrary")),
        name="rw_scan_prompt",
    )(r, d, k, kk, b, v, s0)
    return y, s_t


def _rw_scan_sample(r, d, k, kk, b, v, s0, batch, seq):
    assert seq == GROUP
    n = batch * seq
    tile = pl.BlockSpec((LANES, n), lambda p: (p, 0))
    st = pl.BlockSpec((batch, None, LANES, HD), lambda p: (0, p, 0, 0))
    y, s_t = pl.pallas_call(
        functools.partial(_scan_body, n_groups=batch, per_group_state=True, chunk_axis=None),
        grid=(PAIRS,),
        in_specs=[tile] * 6 + [st],
        out_specs=[tile, st],
        out_shape=[jax.ShapeDtypeStruct(r.shape, F32),
                   jax.ShapeDtypeStruct((batch, PAIRS, LANES, HD), F32)],
        scratch_shapes=[pltpu.VMEM((HD, LANES), F32), pltpu.VMEM((n, LANES), F32),
                        pltpu.VMEM((n, LANES), F32)],
        compiler_params=_params(("arbitrary",)),
        name="rw_scan_sample",
    )(r, d, k, kk, b, v, s0)
    return y, s_t


def _rw_out_body(y_ref, r_ref, k_ref, v_ref, g_ref, x_ref, gate_ref, wo_ref, lng_ref, lnb_ref,
                 rk_ref, o_ref, *, tm):
    y = y_ref[...].reshape(HEADS, HD, tm)
    mean = jnp.mean(y, axis=1, keepdims=True)
    var = jnp.mean(jnp.square(y - mean), axis=1, keepdims=True)
    y = ((y - mean) * lax.rsqrt(var + GN_EPS)).reshape(D, tm) * _lanes(lng_ref[...], tm) \
        + _lanes(lnb_ref[...], tm)
    rk = (r_ref[...] * k_ref[...] * _lanes(rk_ref[...], tm)).reshape(HEADS, HD, tm)
    bonus = jnp.sum(rk, axis=1, keepdims=True) * v_ref[...].reshape(HEADS, HD, tm)
    yg = (y + bonus.reshape(D, tm)) * g_ref[...]
    o = _dot(yg.T.astype(BF16), wo_ref[...])
    o_ref[...] = x_ref[...] + gate_ref[0] * o


def _rw_out(y, r, k, v, g, x, gate, wo, cols, tm, tiles_per_seq):
    n = x.shape[0]
    row = pl.BlockSpec((tm, D), lambda i: (i, 0))
    cm = pl.BlockSpec((D, tm), lambda i: (0, i))
    return pl.pallas_call(
        functools.partial(_rw_out_body, tm=tm),
        grid=(n // tm,),
        in_specs=[cm] * 5 + [row, _mod_spec(gate, tiles_per_seq), _full_spec(wo)]
        + [_full_spec(c) for c in cols],
        out_specs=row,
        out_shape=jax.ShapeDtypeStruct((n, D), F32),
        compiler_params=_params(("arbitrary",)),
        name="rw_out",
    )(y, r, k, v, g, x, gate, wo, *cols)


def _col(p):
    return jnp.broadcast_to(p.reshape(D, 1), (D, LANES))


def kernel(x_prompt, x_sample, cache_k, cache_v, state_wkv, state_shift, page_table, c_prompt, c_sample, norm_mix, ada_w_mix, ada_b_mix, norm_ffn, ada_w_ffn, ada_b_ffn, w_up, w_down, sb_w_qkv, sb_w_o, sb_bias, rw_mu, rw_w_r, rw_w_k, rw_w_v, rw_w_o, rw_w0, rw_w1, rw_w2, rw_a0, rw_a1, rw_a2, rw_g1, rw_g2, rw_k_k, rw_k_a, rw_r_k, rw_ln_g, rw_ln_b, final_norm):
    batch, seq, _ = x_prompt.shape
    dbatch, dseq, _ = x_sample.shape
    depth = norm_mix.shape[0]
    n_p, n_s = batch * seq, dbatch * dseq
    tm_p, tm_s = 512, n_s
    tps_p, tps_s = seq // tm_p, 1

    mod_mix = _ada_mod(jnp.concatenate([c_prompt, c_sample], axis=0), ada_w_mix, ada_b_mix)
    mod_ffn = _ada_mod(jnp.concatenate([c_prompt, c_sample], axis=0), ada_w_ffn, ada_b_ffn)

    def split(mod):
        parts = [mod[:, n * D:(n + 1) * D] for n in range(3)]
        return ([m[:batch].reshape(batch, 1, D) for m in parts],
                [jnp.repeat(m[batch:], dseq, axis=0).reshape(1, n_s, D) for m in parts])

    xp = x_prompt.reshape(n_p, D)
    xs = x_sample.reshape(n_s, D)
    fin = final_norm.reshape(1, D)
    outs = {}

    for i in range(depth):
        j = i // 2
        g_mix = norm_mix[i].reshape(1, D)
        (sh_p, sc_p, ga_p), (sh_s, sc_s, ga_s) = split(mod_mix[i])
        if i % 2 == 0:
            w_qkv = sb_w_qkv[j].astype(BF16)
            w_o = sb_w_o[j].astype(BF16)
            qp, kp, vp, kbp, vbp = _qkv(xp, g_mix, sc_p, sh_p, w_qkv, tm_p, tps_p)
            qs, ks, vs, _, _ = _qkv(xs, g_mix, sc_s, sh_s, w_qkv, tm_s, tps_s)
            op = _sb_prompt(qp, kbp, vbp, sb_bias[j], batch, seq, 256)
            n_phys, page = cache_k.shape[1], cache_k.shape[2]
            os_ = _sb_sample(qs, ks, vs, cache_k[j].reshape(n_phys, page * HEADS, HD),
                             cache_v[j].reshape(n_phys, page * HEADS, HD), page_table, sb_bias[j], dseq)
            xp = _proj_res(op, xp, ga_p, w_o, tm_p, tps_p)
            xs = _proj_res(os_, xs, ga_s, w_o, tm_s, tps_s)
            outs.setdefault("kp", []).append(kp.reshape(batch, seq, HEADS, HD))
            outs.setdefault("vp", []).append(vp.reshape(batch, seq, HEADS, HD))
            outs.setdefault("ks", []).append(ks.reshape(dbatch, dseq, HEADS, HD))
            outs.setdefault("vs", []).append(vs.reshape(dbatch, dseq, HEADS, HD))
        else:
            pad = GATE_LORA_PAD - rw_g1.shape[-1]
            mats = (jnp.pad(rw_mu[j], ((0, 2), (0, 0))),
                    rw_w_r[j].T.astype(BF16), rw_w_k[j].T.astype(BF16), rw_w_v[j].T.astype(BF16),
                    rw_w1[j].astype(BF16), rw_w2[j].T.astype(BF16),
                    rw_a1[j].astype(BF16), rw_a2[j].T.astype(BF16),
                    jnp.pad(rw_g1[j], ((0, 0), (0, pad))).astype(BF16),
                    jnp.pad(rw_g2[j], ((0, pad), (0, 0))).T.astype(BF16))
            cols = (_col(rw_w0[j]), _col(rw_a0[j]), _col(rw_k_k[j]), _col(rw_k_a[j]))
            ocols = (_col(rw_ln_g[j]), _col(rw_ln_b[j]), _col(rw_r_k[j]))
            w_o = rw_w_o[j].astype(BF16)

            hp = _norm(xp, g_mix, sc_p, sh_p, tm_p, tps_p).reshape(batch, seq, D)
            hs = _norm(xs, g_mix, sc_s, sh_s, tm_s, tps_s).reshape(dbatch, dseq, D)
            prev_p = jnp.concatenate([jnp.zeros((batch, 1, D), F32), hp[:, :-1]], axis=1)
            prev_s = jnp.concatenate([state_shift[j][:, None, :], hs[:, :-1]], axis=1)

            rp, dp, kp_, kkp, bp, vp_, gp = _rw_proj(hp.reshape(n_p, D), prev_p.reshape(n_p, D),
                                                     mats, cols, 256)
            rs, ds, ks_, kks, bs, vs_, gs = _rw_proj(hs.reshape(n_s, D), prev_s.reshape(n_s, D),
                                                     mats, cols, n_s)
            yp, wkv_p = _rw_scan_prompt(rp, dp, kp_, kkp, bp, vp_, batch, seq)
            s0 = state_wkv[j].reshape(dbatch, PAIRS, LANES, HD)
            ys, wkv_s = _rw_scan_sample(rs, ds, ks_, kks, bs, vs_, s0, dbatch, dseq)
            xp = _rw_out(yp, rp, kp_, vp_, gp, xp, ga_p, w_o, ocols, 256, seq // 256)
            xs = _rw_out(ys, rs, ks_, vs_, gs, xs, ga_s, w_o, ocols, n_s, 1)
            outs.setdefault("wp", []).append(wkv_p.reshape(batch, HEADS, HD, HD))
            outs.setdefault("sp", []).append(hp[:, -1])
            outs.setdefault("ws", []).append(wkv_s.reshape(dbatch, HEADS, HD, HD))
            outs.setdefault("ss", []).append(hs[:, -1])

        g_ffn = norm_ffn[i].reshape(1, D)
        (sh_p, sc_p, ga_p), (sh_s, sc_s, ga_s) = split(mod_ffn[i])
        wu = w_up[i].astype(BF16)
        wd = w_down[i].astype(BF16)
        last = i == depth - 1
        xp = _mlp(xp, g_ffn, sc_p, sh_p, ga_p, wu, wd, fin, tm_p, tps_p, last)
        xs = _mlp(xs, g_ffn, sc_s, sh_s, ga_s, wu, wd, fin, tm_s, tps_s, last)

    return (xp.reshape(batch, seq, D), xs.reshape(dbatch, dseq, D),
            jnp.stack(outs["kp"]), jnp.stack(outs["vp"]), jnp.stack(outs["ks"]),
            jnp.stack(outs["vs"]), jnp.stack(outs["wp"]), jnp.stack(outs["sp"]),
            jnp.stack(outs["ws"]), jnp.stack(outs["ss"]))
```

```python
import functools

import jax
import jax.numpy as jnp
from jax import lax
from jax.experimental import pallas as pl
from jax.experimental.pallas import tpu as pltpu

F32 = jnp.float32
BF16 = jnp.bfloat16

D = 1024
HEADS = 16
HD = 64
PAIRS = HEADS // 2
LANES = 128
DFF = 4 * D
NORM_EPS = 1e-6
GN_EPS = 1e-5 * HD
SCALE = HD ** -0.5
GATE_LORA_PAD = 256
GROUP = 8
CHUNK = 64
VMEM_LIMIT = 56 * 1024 * 1024

NT = (((1,), (1,)), ((), ()))


def _params(sem):
    return pltpu.CompilerParams(dimension_semantics=sem, vmem_limit_bytes=VMEM_LIMIT)


def _dot(a, b):
    return jnp.dot(a, b, preferred_element_type=F32)


def _dot_nt(a, b):
    return lax.dot_general(a, b, NT, preferred_element_type=F32)


def _softplus(x):
    return jnp.maximum(x, 0.0) + jnp.log(1.0 + jnp.exp(-jnp.abs(x)))


def _adanorm(x, g, scale, shift):
    y = x * lax.rsqrt(jnp.mean(x * x, axis=-1, keepdims=True) + NORM_EPS)
    return (y * g) * (1.0 + scale) + shift


def _mod_spec(mod, tiles_per_seq):
    rows = mod.shape[1]
    return pl.BlockSpec((1, rows, D), lambda i: (i // tiles_per_seq, 0, 0))


def _full_spec(a):
    nd = a.ndim
    return pl.BlockSpec(a.shape, lambda i: (0,) * nd, pipeline_mode=pl.Buffered(1))


def _ada_body(c_ref, w_ref, b_ref, o_ref):
    c = c_ref[...]
    s = (c * jax.nn.sigmoid(c)).astype(BF16)
    o_ref[0] = _dot(s, w_ref[0].astype(BF16)) + b_ref[0]


def _ada_mod(c_all, w, b):
    layers, rows, tn = w.shape[0], c_all.shape[0], 768
    return pl.pallas_call(
        _ada_body,
        grid=(layers, 3 * D // tn),
        in_specs=[pl.BlockSpec((rows, D), lambda l, j: (0, 0)),
                  pl.BlockSpec((1, D, tn), lambda l, j: (l, 0, j)),
                  pl.BlockSpec((1, 1, tn), lambda l, j: (l, 0, j))],
        out_specs=pl.BlockSpec((1, rows, tn), lambda l, j: (l, 0, j)),
        out_shape=jax.ShapeDtypeStruct((layers, rows, 3 * D), F32),
        compiler_params=_params(("arbitrary", "arbitrary")),
        name="ada_mod",
    )(c_all, w, b.reshape(layers, 1, 3 * D))


def _qkv_body(x_ref, g_ref, sc_ref, sh_ref, wq_ref, wkt_ref, wvt_ref, wk_ref, wv_ref,
              q_ref, kt_ref, vt_ref, ktb_ref, k_ref, v_ref):
    h = _adanorm(x_ref[...], g_ref[...], sc_ref[0], sh_ref[0]).astype(BF16)
    q_ref[...] = _dot(h, wq_ref[...]) * SCALE
    kt = _dot_nt(wkt_ref[...], h)
    kt_ref[...] = kt
    ktb_ref[...] = kt.astype(BF16)
    vt_ref[...] = _dot_nt(wvt_ref[...], h)
    k_ref[...] = _dot(h, wk_ref[...]).astype(k_ref.dtype)
    v_ref[...] = _dot(h, wv_ref[...]).astype(v_ref.dtype)


def _qkv(x, g, scale, shift, w, tm, tiles_per_seq, row_dtype):
    n = x.shape[0]
    seq = tm * tiles_per_seq
    wq, wk, wv = w[:, 0:D], w[:, D:2 * D], w[:, 2 * D:3 * D]
    row = pl.BlockSpec((tm, D), lambda i: (i, 0))
    cm = pl.BlockSpec((None, D, tm), lambda i: (i // tiles_per_seq, 0, i % tiles_per_seq))
    return pl.pallas_call(
        _qkv_body,
        grid=(n // tm,),
        in_specs=[row, _full_spec(g), _mod_spec(scale, tiles_per_seq),
                  _mod_spec(shift, tiles_per_seq)] + [_full_spec(wq)] * 5,
        out_specs=[row, cm, cm, cm, row, row],
        out_shape=[jax.ShapeDtypeStruct((n, D), F32),
                   jax.ShapeDtypeStruct((n // seq, D, seq), F32),
                   jax.ShapeDtypeStruct((n // seq, D, seq), F32),
                   jax.ShapeDtypeStruct((n // seq, D, seq), BF16),
                   jax.ShapeDtypeStruct((n, D), row_dtype),
                   jax.ShapeDtypeStruct((n, D), row_dtype)],
        compiler_params=_params(("arbitrary",)),
        name="qkv",
    )(x, g, scale, shift, wq, wk.T, wv.T, wk, wv)


def _proj_res_body(o_ref, x_ref, gate_ref, w_ref, y_ref):
    y_ref[...] = x_ref[...] + gate_ref[0] * _dot(o_ref[...].astype(BF16), w_ref[...])


def _proj_res(o, x, gate, w, tm, tiles_per_seq):
    n = x.shape[0]
    row = pl.BlockSpec((tm, D), lambda i: (i, 0))
    return pl.pallas_call(
        _proj_res_body,
        grid=(n // tm,),
        in_specs=[row, row, _mod_spec(gate, tiles_per_seq), _full_spec(w)],
        out_specs=row,
        out_shape=jax.ShapeDtypeStruct((n, D), F32),
        compiler_params=_params(("arbitrary",)),
        name="proj_res",
    )(o, x, gate, w)


def _mlp_body(x_ref, g_ref, sc_ref, sh_ref, gate_ref, wu_ref, wd_ref, fin_ref, y_ref,
              *, final_norm):
    x = x_ref[...]
    h = _adanorm(x, g_ref[...], sc_ref[0], sh_ref[0]).astype(BF16)
    acc = jnp.zeros_like(x)
    for c in range(DFF // D):
        u = jnp.maximum(_dot(h, wu_ref[:, c * D:(c + 1) * D]), 0.0)
        acc = acc + _dot((u * u).astype(BF16), wd_ref[c * D:(c + 1) * D, :])
    y = x + gate_ref[0] * acc
    if final_norm:
        y = (y * lax.rsqrt(jnp.mean(y * y, axis=-1, keepdims=True) + NORM_EPS)) * fin_ref[...]
    y_ref[...] = y


def _mlp(x, g, scale, shift, gate, wu, wd, fin, tm, tiles_per_seq, final_norm):
    n = x.shape[0]
    row = pl.BlockSpec((tm, D), lambda i: (i, 0))
    return pl.pallas_call(
        functools.partial(_mlp_body, final_norm=final_norm),
        grid=(n // tm,),
        in_specs=[row, _full_spec(g), _mod_spec(scale, tiles_per_seq),
                  _mod_spec(shift, tiles_per_seq), _mod_spec(gate, tiles_per_seq),
                  _full_spec(wu), _full_spec(wd), _full_spec(fin)],
        out_specs=row,
        out_shape=jax.ShapeDtypeStruct((n, D), F32),
        compiler_params=_params(("arbitrary",)),
        name="mlp",
    )(x, g, scale, shift, gate, wu, wd, fin)


def _norm_body(x_ref, g_ref, sc_ref, sh_ref, h_ref):
    h_ref[...] = _adanorm(x_ref[...], g_ref[...], sc_ref[0], sh_ref[0])


def _norm(x, g, scale, shift, tm, tiles_per_seq):
    n = x.shape[0]
    row = pl.BlockSpec((tm, D), lambda i: (i, 0))
    return pl.pallas_call(
        _norm_body,
        grid=(n // tm,),
        in_specs=[row, _full_spec(g), _mod_spec(scale, tiles_per_seq),
                  _mod_spec(shift, tiles_per_seq)],
        out_specs=row,
        out_shape=jax.ShapeDtypeStruct((n, D), F32),
        compiler_params=_params(("arbitrary",)),
        name="ada_norm",
    )(x, g, scale, shift)


def _suffix_ones():
    j = lax.broadcasted_iota(jnp.int32, (2 * LANES, 2 * LANES), 0) % LANES
    s = lax.broadcasted_iota(jnp.int32, (2 * LANES, 2 * LANES), 1)
    return jnp.where((j >= s) | (s >= LANES), 1.0, 0.0).astype(BF16)


def _neg_abs(x):
    bits = lax.bitcast_convert_type(x, jnp.uint32) | jnp.uint32(0x80000000)
    return lax.bitcast_convert_type(bits, F32)


def _split(x):
    hi = x.astype(BF16)
    return hi, (x - hi.astype(F32)).astype(BF16)


def _rows(x, y):
    return tuple(jnp.concatenate([a, b], axis=0) for a, b in zip(x, y))


def _dot3(x, y):
    (xh, xl), (yh, yl) = x, y
    return _dot(jnp.concatenate([xh, xh, xl], axis=1), jnp.concatenate([yh, yl, yh], axis=0))


def _dot3_nt(x, y):
    (xh, xl), (yh, yl) = x, y
    return _dot_nt(jnp.concatenate([xh, xh, xl], axis=1), jnp.concatenate([yh, yl, yh], axis=1))


def _sb_weights(z, mask, uo, carry):
    n = z.shape[1] // LANES
    sp = jnp.maximum(z, 0.0) + jnp.log(1.0 + jnp.exp(_neg_abs(z)))
    if mask is not None:
        sp = jnp.where(mask, sp, 0.0)
    hi, lo = _split(sp)
    parts = [None] * n
    for s in reversed(range(n)):
        sl = slice(s * LANES, (s + 1) * LANES)
        cs = _dot(jnp.concatenate([hi[:, sl], lo[:, sl]], axis=1), uo)
        parts[s] = jnp.exp(z[:, sl] - cs[:, :LANES] - carry)
        carry = carry + cs[:, LANES:]
    a = parts[0] if n == 1 else jnp.concatenate(parts, axis=1)
    if mask is not None:
        a = jnp.where(mask, a, 0.0)
    return a, carry


def _sbp_body(bias_ref, q_ref, k_ref, v_ref, uo_ref, o_ref, acc_ref, car_ref, *, tq):
    p = pl.program_id(1)
    i = pl.program_id(2)
    wide = 2 * tq
    lane = lax.broadcasted_iota(jnp.int32, (tq, LANES), 1)
    q = q_ref[...]
    qh = (jnp.where(lane < HD, q, 0.0).astype(BF16), jnp.where(lane >= HD, q, 0.0).astype(BF16))
    acc_ref[...] = jnp.zeros_like(acc_ref)
    car_ref[...] = jnp.zeros_like(car_ref)
    uo = uo_ref[...]

    def block(off, width, masked):
        kt = k_ref[:, pl.ds(off, width)]
        v = v_ref[pl.ds(off, width), :]
        mask = None
        if masked:
            mask = (lax.broadcasted_iota(jnp.int32, (tq, width), 1)
                    < lax.broadcasted_iota(jnp.int32, (tq, width), 0))
        for hh in range(2):
            z = _dot(qh[hh], kt) + bias_ref[2 * p + hh]
            a, car = _sb_weights(z, mask, uo, car_ref[hh])
            acc_ref[hh] += _dot(a.astype(BF16), v)
            car_ref[hh] = car

    block(pl.multiple_of(i * tq, tq), tq, True)

    @pl.when(i % 2 == 1)
    def _():
        block(pl.multiple_of((i - 1) * tq, tq), tq, False)

    def body(n, c):
        block(pl.multiple_of((i // 2 - 1 - n) * wide, wide), wide, False)
        return c

    lax.fori_loop(0, i // 2, body, 0)
    o_ref[...] = jnp.where(lane < HD, acc_ref[0], acc_ref[1]).astype(o_ref.dtype)


def _sb_prompt(q, ktb, vb, bias, batch, seq, tq):
    n = q.shape[0]
    nq = seq // tq
    return pl.pallas_call(
        functools.partial(_sbp_body, tq=tq),
        grid=(batch, PAIRS, nq),
        in_specs=[pl.BlockSpec(memory_space=pltpu.SMEM),
                  pl.BlockSpec((tq, LANES), lambda b, p, i: (b * nq + i, p)),
                  pl.BlockSpec((None, LANES, seq), lambda b, p, i: (b, p, 0)),
                  pl.BlockSpec((seq, LANES), lambda b, p, i: (b, p)),
                  pl.BlockSpec((2 * LANES, 2 * LANES), lambda b, p, i: (0, 0))],
        out_specs=pl.BlockSpec((tq, LANES), lambda b, p, i: (b * nq + i, p)),
        out_shape=jax.ShapeDtypeStruct((n, D), BF16),
        scratch_shapes=[pltpu.VMEM((2, tq, LANES), F32), pltpu.VMEM((2, tq, LANES), F32)],
        compiler_params=_params(("arbitrary", "arbitrary", "arbitrary")),
        name="sb_prompt",
    )(bias, q, ktb, vb, _suffix_ones())


def _sbs_body(pt_ref, q_ref, knt_ref, vnt_ref, *rest, n_new, n_pages, group):
    del pt_ref
    kc_refs, vc_refs = rest[:group], rest[group:2 * group]
    bias_ref, uo_ref, o_ref, qb_ref, acc_ref, car_ref = rest[2 * group:]
    b = pl.program_id(0)
    p = pl.program_id(1)
    row = lax.broadcasted_iota(jnp.int32, (LANES, D), 0)
    col = lax.broadcasted_iota(jnp.int32, (LANES, D), 1)
    own = (row // n_new) == (col // HD)
    uo = uo_ref[...]

    def sweep(kts, vts, mask):
        z = jnp.concatenate([_dot(qb_ref[...], kt.astype(BF16)) + bias_ref[...] for kt in kts], axis=1)
        a, car = _sb_weights(z, mask, uo, car_ref[...])
        car_ref[...] = car
        a = a.astype(BF16)
        o = _dot_nt(a[:, 0:LANES], vts[0].astype(BF16))
        for r in range(1, len(vts)):
            o = o + _dot_nt(a[:, r * LANES:(r + 1) * LANES], vts[r].astype(BF16))
        acc_ref[...] += o

    @pl.when(p == 0)
    def _():
        q = q_ref[...]
        qt = jnp.broadcast_to(q[None], (HEADS, n_new, D)).reshape(LANES, D)
        qb_ref[...] = jnp.where(own, qt, 0.0).astype(BF16)
        acc_ref[...] = jnp.zeros_like(acc_ref)
        car_ref[...] = jnp.zeros_like(car_ref)
        off = (b % (LANES // n_new)) * n_new
        r = lax.broadcasted_iota(jnp.int32, (LANES, LANES), 0)
        s = lax.broadcasted_iota(jnp.int32, (LANES, LANES), 1) - off
        sweep([knt_ref[...]], [vnt_ref[...]], (s >= 0) & (s < (r % n_new)))

    sweep([r[...] for r in kc_refs], [r[...] for r in vc_refs], None)

    @pl.when(p == n_pages // group - 1)
    def _():
        o = jnp.where(own, acc_ref[...], 0.0).reshape(HEADS, n_new, D)
        o_ref[...] = jnp.sum(o, axis=0)


def _sb_sample(q, knt, vnt, cache_kt, cache_vt, page_table, bias, n_new, group=4):
    n = q.shape[0]
    batch, n_pages = page_table.shape
    page = cache_kt.shape[2]
    assert page == LANES and HEADS * n_new == LANES and n_pages % group == 0 and n % LANES == 0
    bias_rows = jnp.broadcast_to(jnp.repeat(bias, n_new)[:, None], (LANES, LANES))
    per_tile = LANES // n_new
    new = pl.BlockSpec((n_new, D), lambda b, p, pt: (b, 0))
    newt = pl.BlockSpec((D, LANES), lambda b, p, pt: (0, b // per_tile))

    def cache(r):
        return pl.BlockSpec((None, D, page),
                            lambda b, p, pt: (pt[b, n_pages - group * (p + 1) + r], 0, 0))

    return pl.pallas_call(
        functools.partial(_sbs_body, n_new=n_new, n_pages=n_pages, group=group),
        grid_spec=pltpu.PrefetchScalarGridSpec(
            num_scalar_prefetch=1,
            grid=(batch, n_pages // group),
            in_specs=[new, newt, newt] + [cache(r) for r in range(group)] * 2
            + [pl.BlockSpec((LANES, LANES), lambda b, p, pt: (0, 0)),
               pl.BlockSpec((2 * LANES, 2 * LANES), lambda b, p, pt: (0, 0))],
            out_specs=new,
            scratch_shapes=[pltpu.VMEM((LANES, D), BF16), pltpu.VMEM((LANES, D), F32),
                            pltpu.VMEM((LANES, LANES), F32)]),
        out_shape=jax.ShapeDtypeStruct((n, D), F32),
        compiler_params=_params(("arbitrary", "arbitrary")),
        name="sb_sample",
    )(page_table, q, knt, vnt, *([cache_kt] * group), *([cache_vt] * group), bias_rows, _suffix_ones())


def _lanes(col, tm):
    return jnp.concatenate([col] * (tm // LANES), axis=1)


def _rw_proj_body(h_ref, prev_ref, mu_ref, wr_ref, wk_ref, wv_ref, w1_ref, w2_ref, a1_ref,
                  a2_ref, g1_ref, g2_ref, w0_ref, a0_ref, kk_ref, ka_ref,
                  r_out, d_out, lw_out, k_out, kkn_out, b_out, v_out, g_out, *, tm):
    h = h_ref[...]
    xx = prev_ref[...] - h
    xr, xw, xk, xv, xa, xg = ((h + xx * mu_ref[n:n + 1, :]).astype(BF16) for n in range(6))
    r_out[...] = _dot_nt(wr_ref[...], xr)
    k = _dot_nt(wk_ref[...], xk)
    v_out[...] = _dot_nt(wv_ref[...], xv)
    w_lin = _lanes(w0_ref[...], tm) + _dot_nt(w2_ref[...], jnp.tanh(_dot(xw, w1_ref[...])).astype(BF16))
    w = -_softplus(-w_lin) - 0.5
    log_decay = -jnp.exp(w)
    lw_out[...] = log_decay
    d_out[...] = jnp.exp(log_decay)
    a = jax.nn.sigmoid(_lanes(a0_ref[...], tm) + _dot_nt(a2_ref[...], _dot(xa, a1_ref[...]).astype(BF16)))
    g_out[...] = _dot_nt(g2_ref[...], jax.nn.sigmoid(_dot(xg, g1_ref[...])).astype(BF16))
    kk = (k * _lanes(kk_ref[...], tm)).reshape(HEADS, HD, tm)
    norm = jnp.sqrt(jnp.sum(kk * kk, axis=1, keepdims=True))
    kk = (kk / jnp.maximum(norm, 1e-12)).reshape(D, tm)
    kkn_out[...] = kk
    b_out[...] = kk * a
    k_out[...] = k * (1.0 + (a - 1.0) * _lanes(ka_ref[...], tm))


def _rw_proj(h, prev, mats, cols, tm):
    n = h.shape[0]
    row = pl.BlockSpec((tm, D), lambda i: (i, 0))
    out = pl.BlockSpec((D, tm), lambda i: (0, i))
    return pl.pallas_call(
        functools.partial(_rw_proj_body, tm=tm),
        grid=(n // tm,),
        in_specs=[row, row] + [_full_spec(m) for m in mats] + [_full_spec(c) for c in cols],
        out_specs=[out] * 8,
        out_shape=[jax.ShapeDtypeStruct((D, n), F32)] * 8,
        compiler_params=_params(("arbitrary",)),
        name="rw_proj",
    )(h, prev, *mats, *cols)


def _chunk_body(r_ref, lw_ref, k_ref, kk_ref, b_ref, v_ref, y_ref, sT_ref, sbd_ref, ybuf_ref,
                *, n_chunks):
    c = pl.program_id(2)

    @pl.when(c == 0)
    def _():
        sbd_ref[...] = jnp.zeros_like(sbd_ref)

    m_a = lax.broadcasted_iota(jnp.int32, (CHUNK, LANES), 1) < HD
    rr = lax.broadcasted_iota(jnp.int32, (LANES, LANES), 0)
    cc = lax.broadcasted_iota(jnp.int32, (LANES, LANES), 1)
    same = (rr // CHUNK) == (cc // CHUNK)
    strict = same & (cc < rr)
    incl = same & (cc <= rr)
    ident = rr == cc
    tr = lax.broadcasted_iota(jnp.int32, (CHUNK, CHUNK), 0)
    tc = lax.broadcasted_iota(jnp.int32, (CHUNK, CHUNK), 1)
    tri = jnp.where(tc <= tr, 1.0, 0.0).astype(BF16)

    def stack(x):
        return jnp.concatenate([jnp.where(m_a, x, 0.0), jnp.where(m_a, 0.0, x)], axis=0)

    tiles = [ref[...].T for ref in (r_ref, lw_ref, k_ref, kk_ref, b_ref, v_ref)]

    ch = range(n_chunks)
    sls = [slice(n * CHUNK, (n + 1) * CHUNK) for n in ch]
    r, lw, k, kk, b, v = ([x[sl] for sl in sls] for x in tiles)
    lw_parts = [_split(x) for x in lw]
    cum = [_dot(tri, hi) + _dot(tri, lo) for hi, lo in lw_parts]
    cum_end = [x[CHUNK - 1:CHUNK, :] for x in cum]
    e_neg = [jnp.exp(-x) for x in cum]
    e_end = [jnp.exp(ce - x) for ce, x in zip(cum_end, cum)]
    kap = [_split(stack(kk[n] * jnp.exp(cum[n] - lw[n]))) for n in ch]
    rti = [_split(stack(r[n] * jnp.exp(cum[n]))) for n in ch]
    v_s = [_split(stack(x)) for x in v]
    bet = [_split(stack(b[n] * e_neg[n])) for n in ch]
    kti = [_split(stack(k[n] * e_neg[n])) for n in ch]
    gram = [_dot3_nt(_rows(kap[n], rti[n]), _rows(bet[n], kti[n])) for n in ch]
    a_k = [_split(jnp.where(strict, x[0:LANES, LANES:], 0.0)) for x in gram]
    p_k = [_split(jnp.where(incl, x[LANES:, LANES:], 0.0)) for x in gram]
    p_b = [_split(jnp.where(incl, -x[LANES:, 0:LANES], 0.0)) for x in gram]

    a_b = [jnp.where(strict, x[0:LANES, 0:LANES], 0.0) for x in gram]
    t_inv = [jnp.where(ident, 1.0, jnp.where((rr // 2) == (cc // 2), -x, 0.0)) for x in a_b]
    size = 2
    while size < CHUNK:
        join = ((rr // (2 * size)) == (cc // (2 * size))) & ((rr // size) != (cc // size))
        t_parts = [_split(x) for x in t_inv]
        tb = [_dot3(t_parts[n], _split(jnp.where(join, a_b[n], 0.0))) for n in ch]
        t_inv = [t_inv[n] - _dot3(_split(tb[n]), t_parts[n]) for n in ch]
        size *= 2
    t_low = [_split(jnp.where(ident, 0.0, x)) for x in t_inv]

    kap_f = [stack(kk[n] * jnp.exp(cum[n] - lw[n])) for n in ch]
    w = [_split(kap_f[n] + _dot3(t_low[n], kap[n])) for n in ch]
    av = [_dot3(a_k[n], v_s[n]) for n in ch]
    u = [av[n] + _dot3(t_low[n], _split(av[n])) for n in ch]
    u_s = [_split(x) for x in u]
    khat_t = [_split(stack(k[n] * e_end[n]).T) for n in ch]
    bhat_t = [_split(stack(b[n] * e_end[n]).T) for n in ch]
    m = [_split(_dot3(bhat_t[n], w[n])) for n in ch]
    const = [_dot3(khat_t[n], v_s[n]) - _dot3(bhat_t[n], u_s[n]) for n in ch]
    g_col = [jnp.broadcast_to(jnp.exp(x), (LANES, LANES)).T for x in cum_end]

    sbd = sbd_ref[...]
    for n in ch:
        s_s = _split(sbd)
        d_s = _split(_dot3(w[n], s_s) + u[n])
        y_s = _dot3(rti[n], s_s) + _dot3(p_k[n], v_s[n]) + _dot3(p_b[n], d_s)
        ybuf_ref[sls[n], :] = y_s[0:CHUNK] + y_s[CHUNK:]
        sbd = g_col[n] * sbd - _dot3(m[n], s_s) + const[n]
    sbd_ref[...] = sbd

    y_ref[...] = ybuf_ref[...].T

    @pl.when(c == pl.num_programs(2) - 1)
    def _():
        s_t = sbd.T
        top = lax.broadcasted_iota(jnp.int32, (LANES, HD), 0) < HD
        sT_ref[...] = jnp.where(top, s_t[:, 0:HD], s_t[:, HD:])


def _rw_chunk_prompt(r, lw, k, kk, b, v, batch, seq, n_chunks=8):
    width = n_chunks * CHUNK
    assert CHUNK == HD and seq % width == 0 and width % LANES == 0
    nc = seq // width
    tile = pl.BlockSpec((LANES, width), lambda p, bb, c: (p, bb * nc + c))
    st = pl.BlockSpec((None, None, LANES, HD), lambda p, bb, c: (bb, p, 0, 0))
    return pl.pallas_call(
        functools.partial(_chunk_body, n_chunks=n_chunks),
        grid=(PAIRS, batch, nc),
        in_specs=[tile] * 6,
        out_specs=[tile, st],
        out_shape=[jax.ShapeDtypeStruct(r.shape, F32),
                   jax.ShapeDtypeStruct((batch, PAIRS, LANES, HD), F32)],
        scratch_shapes=[pltpu.VMEM((LANES, LANES), F32), pltpu.VMEM((width, LANES), F32)],
        compiler_params=_params(("arbitrary", "arbitrary", "arbitrary")),
        name="rw_chunk_prompt",
    )(r, lw, k, kk, b, v)


def _scan_body(r_ref, d_ref, k_ref, kk_ref, b_ref, v_ref, s0_ref, y_ref, sT_ref,
               vbuf_ref, ybuf_ref, *, n_groups):
    first = lax.broadcasted_iota(jnp.int32, (HD, LANES), 1) < HD
    vbuf_ref[...] = v_ref[...].T

    def group(g, carry):
        base = pl.multiple_of((g // (LANES // GROUP)) * LANES, LANES)
        gl = g % (LANES // GROUP)
        shift = jnp.where(gl == 0, 0, LANES - gl * GROUP)
        tiles = [pltpu.roll(ref[:, pl.ds(base, LANES)], shift, 1)
                 for ref in (d_ref, kk_ref, b_ref, k_ref, r_ref)]
        s_t = s0_ref[g].T
        for s in range(GROUP):
            dc, kkc, bc, kc, rc = (jnp.where(first, x[0:HD, s:s + 1], x[HD:2 * HD, s:s + 1])
                                   for x in tiles)
            t = g * GROUP + s
            sa = jnp.sum(s_t * kkc, axis=0, keepdims=True)
            s_t = s_t * dc - bc * sa + kc * vbuf_ref[pl.ds(t, 1), :]
            ybuf_ref[pl.ds(t, 1), :] = jnp.sum(s_t * rc, axis=0, keepdims=True)
        sT_ref[g] = s_t.T
        return carry

    lax.fori_loop(0, n_groups, group, 0)
    y_ref[...] = ybuf_ref[...].T


def _rw_scan_sample(r, d, k, kk, b, v, s0, batch, seq):
    assert seq == GROUP
    n = batch * seq
    tile = pl.BlockSpec((LANES, n), lambda p: (p, 0))
    st = pl.BlockSpec((batch, None, LANES, HD), lambda p: (0, p, 0, 0))
    return pl.pallas_call(
        functools.partial(_scan_body, n_groups=batch),
        grid=(PAIRS,),
        in_specs=[tile] * 6 + [st],
        out_specs=[tile, st],
        out_shape=[jax.ShapeDtypeStruct(r.shape, F32),
                   jax.ShapeDtypeStruct((batch, PAIRS, LANES, HD), F32)],
        scratch_shapes=[pltpu.VMEM((n, LANES), F32), pltpu.VMEM((n, LANES), F32)],
        compiler_params=_params(("arbitrary",)),
        name="rw_scan_sample",
    )(r, d, k, kk, b, v, s0)


def _rw_out_body(y_ref, r_ref, k_ref, v_ref, g_ref, x_ref, gate_ref, wo_ref, lng_ref, lnb_ref,
                 rk_ref, o_ref, *, tm):
    y = y_ref[...].reshape(HEADS, HD, tm)
    mean = jnp.mean(y, axis=1, keepdims=True)
    var = jnp.mean(jnp.square(y - mean), axis=1, keepdims=True)
    y = ((y - mean) * lax.rsqrt(var + GN_EPS)).reshape(D, tm) * _lanes(lng_ref[...], tm) \
        + _lanes(lnb_ref[...], tm)
    rk = (r_ref[...] * k_ref[...] * _lanes(rk_ref[...], tm)).reshape(HEADS, HD, tm)
    bonus = jnp.sum(rk, axis=1, keepdims=True) * v_ref[...].reshape(HEADS, HD, tm)
    yg = (y + bonus.reshape(D, tm)) * g_ref[...]
    o = _dot(yg.T.astype(BF16), wo_ref[...])
    o_ref[...] = x_ref[...] + gate_ref[0] * o


def _rw_out(y, r, k, v, g, x, gate, wo, cols, tm, tiles_per_seq):
    n = x.shape[0]
    row = pl.BlockSpec((tm, D), lambda i: (i, 0))
    cm = pl.BlockSpec((D, tm), lambda i: (0, i))
    return pl.pallas_call(
        functools.partial(_rw_out_body, tm=tm),
        grid=(n // tm,),
        in_specs=[cm] * 5 + [row, _mod_spec(gate, tiles_per_seq), _full_spec(wo)]
        + [_full_spec(c) for c in cols],
        out_specs=row,
        out_shape=jax.ShapeDtypeStruct((n, D), F32),
        compiler_params=_params(("arbitrary",)),
        name="rw_out",
    )(y, r, k, v, g, x, gate, wo, *cols)


def _col(p):
    return jnp.broadcast_to(p.reshape(D, 1), (D, LANES))


def kernel(x_prompt, x_sample, cache_k, cache_v, state_wkv, state_shift, page_table, c_prompt, c_sample, norm_mix, ada_w_mix, ada_b_mix, norm_ffn, ada_w_ffn, ada_b_ffn, w_up, w_down, sb_w_qkv, sb_w_o, sb_bias, rw_mu, rw_w_r, rw_w_k, rw_w_v, rw_w_o, rw_w0, rw_w1, rw_w2, rw_a0, rw_a1, rw_a2, rw_g1, rw_g2, rw_k_k, rw_k_a, rw_r_k, rw_ln_g, rw_ln_b, final_norm):
    batch, seq, _ = x_prompt.shape
    dbatch, dseq, _ = x_sample.shape
    depth = norm_mix.shape[0]
    n_p, n_s = batch * seq, dbatch * dseq
    tm_p, tm_s = 512, n_s
    tps_p, tps_s = seq // tm_p, 1

    mod_mix = _ada_mod(jnp.concatenate([c_prompt, c_sample], axis=0), ada_w_mix, ada_b_mix)
    mod_ffn = _ada_mod(jnp.concatenate([c_prompt, c_sample], axis=0), ada_w_ffn, ada_b_ffn)

    def split(mod):
        parts = [mod[:, n * D:(n + 1) * D] for n in range(3)]
        return ([m[:batch].reshape(batch, 1, D) for m in parts],
                [jnp.repeat(m[batch:], dseq, axis=0).reshape(1, n_s, D) for m in parts])

    xp = x_prompt.reshape(n_p, D)
    xs = x_sample.reshape(n_s, D)
    fin = final_norm.reshape(1, D)
    outs = {}

    for i in range(depth):
        j = i // 2
        g_mix = norm_mix[i].reshape(1, D)
        (sh_p, sc_p, ga_p), (sh_s, sc_s, ga_s) = split(mod_mix[i])
        if i % 2 == 0:
            w_qkv = sb_w_qkv[j].astype(BF16)
            w_o = sb_w_o[j].astype(BF16)
            qp, ktp, vtp, ktbp, _, vbp = _qkv(xp, g_mix, sc_p, sh_p, w_qkv, tm_p, tps_p, BF16)
            qs, kts, vts, _, ks, vs = _qkv(xs, g_mix, sc_s, sh_s, w_qkv, tm_s, tps_s, F32)
            op = _sb_prompt(qp, ktbp, vbp, sb_bias[j], batch, seq, 256)
            n_phys, page = cache_k.shape[1], cache_k.shape[2]
            cache_kt = jnp.transpose(cache_k[j], (0, 2, 3, 1)).reshape(n_phys, D, page)
            cache_vt = jnp.transpose(cache_v[j], (0, 2, 3, 1)).reshape(n_phys, D, page)
            os_ = _sb_sample(qs, kts[0], vts[0], cache_kt, cache_vt, page_table, sb_bias[j], dseq)
            xp = _proj_res(op, xp, ga_p, w_o, tm_p, tps_p)
            xs = _proj_res(os_, xs, ga_s, w_o, tm_s, tps_s)
            outs.setdefault("kp", []).append(
                jnp.transpose(ktp.reshape(batch, HEADS, HD, seq), (0, 3, 1, 2)))
            outs.setdefault("vp", []).append(
                jnp.transpose(vtp.reshape(batch, HEADS, HD, seq), (0, 3, 1, 2)))
            outs.setdefault("ks", []).append(ks.reshape(dbatch, dseq, HEADS, HD))
            outs.setdefault("vs", []).append(vs.reshape(dbatch, dseq, HEADS, HD))
        else:
            pad = GATE_LORA_PAD - rw_g1.shape[-1]
            mats = (jnp.pad(rw_mu[j], ((0, 2), (0, 0))),
                    rw_w_r[j].T.astype(BF16), rw_w_k[j].T.astype(BF16), rw_w_v[j].T.astype(BF16),
                    rw_w1[j].astype(BF16), rw_w2[j].T.astype(BF16),
                    rw_a1[j].astype(BF16), rw_a2[j].T.astype(BF16),
                    jnp.pad(rw_g1[j], ((0, 0), (0, pad))).astype(BF16),
                    jnp.pad(rw_g2[j], ((0, pad), (0, 0))).T.astype(BF16))
            cols = (_col(rw_w0[j]), _col(rw_a0[j]), _col(rw_k_k[j]), _col(rw_k_a[j]))
            ocols = (_col(rw_ln_g[j]), _col(rw_ln_b[j]), _col(rw_r_k[j]))
            w_o = rw_w_o[j].astype(BF16)

            hp = _norm(xp, g_mix, sc_p, sh_p, tm_p, tps_p).reshape(batch, seq, D)
            hs = _norm(xs, g_mix, sc_s, sh_s, tm_s, tps_s).reshape(dbatch, dseq, D)
            prev_p = jnp.concatenate([jnp.zeros((batch, 1, D), F32), hp[:, :-1]], axis=1)
            prev_s = jnp.concatenate([state_shift[j][:, None, :], hs[:, :-1]], axis=1)

            rp, _, lwp, kp_, kkp, bp, vp_, gp = _rw_proj(hp.reshape(n_p, D), prev_p.reshape(n_p, D),
                                                         mats, cols, 256)
            rs, ds, _, ks_, kks, bs, vs_, gs = _rw_proj(hs.reshape(n_s, D), prev_s.reshape(n_s, D),
                                                        mats, cols, n_s)
            yp, wkv_p = _rw_chunk_prompt(rp, lwp, kp_, kkp, bp, vp_, batch, seq)
            s0 = state_wkv[j].reshape(dbatch, PAIRS, LANES, HD)
            ys, wkv_s = _rw_scan_sample(rs, ds, ks_, kks, bs, vs_, s0, dbatch, dseq)
            xp = _rw_out(yp, rp, kp_, vp_, gp, xp, ga_p, w_o, ocols, 256, seq // 256)
            xs = _rw_out(ys, rs, ks_, vs_, gs, xs, ga_s, w_o, ocols, n_s, 1)
            outs.setdefault("wp", []).append(wkv_p.reshape(batch, HEADS, HD, HD))
            outs.setdefault("sp", []).append(hp[:, -1])
            outs.setdefault("ws", []).append(wkv_s.reshape(dbatch, HEADS, HD, HD))
            outs.setdefault("ss", []).append(hs[:, -1])

        g_ffn = norm_ffn[i].reshape(1, D)
        (sh_p, sc_p, ga_p), (sh_s, sc_s, ga_s) = split(mod_ffn[i])
        wu = w_up[i].astype(BF16)
        wd = w_down[i].astype(BF16)
        last = i == depth - 1
        xp = _mlp(xp, g_ffn, sc_p, sh_p, ga_p, wu, wd, fin, tm_p, tps_p, last)
        xs = _mlp(xs, g_ffn, sc_s, sh_s, ga_s, wu, wd, fin, tm_s, tps_s, last)

    return (xp.reshape(batch, seq, D), xs.reshape(dbatch, dseq, D),
            jnp.stack(outs["kp"]), jnp.stack(outs["vp"]), jnp.stack(outs["ks"]),
            jnp.stack(outs["vs"]), jnp.stack(outs["wp"]), jnp.stack(outs["sp"]),
            jnp.stack(outs["ws"]), jnp.stack(outs["ss"]))
```

```python
import functools

import jax
import jax.numpy as jnp
from jax import lax
from jax.experimental import pallas as pl
from jax.experimental.pallas import tpu as pltpu

F32 = jnp.float32
BF16 = jnp.bfloat16

D = 1024
HEADS = 16
HD = 64
PAIRS = HEADS // 2
LANES = 128
DFF = 4 * D
NORM_EPS = 1e-6
GN_EPS = 1e-5 * HD
LOG2E = 1.4426950408889634
Q_SCALE = HD ** -0.5 * LOG2E
GATE_LORA_PAD = 256
GROUP = 8
CHUNK = 64
VMEM_LIMIT = 56 * 1024 * 1024

NT = (((1,), (1,)), ((), ()))


def _params(sem):
    return pltpu.CompilerParams(dimension_semantics=sem, vmem_limit_bytes=VMEM_LIMIT)


def _dot(a, b):
    return jnp.dot(a, b, preferred_element_type=F32)


def _dot_nt(a, b):
    return lax.dot_general(a, b, NT, preferred_element_type=F32)


def _softplus(x):
    return jnp.maximum(x, 0.0) + jnp.log(1.0 + jnp.exp(-jnp.abs(x)))


def _adanorm(x, g, scale, shift):
    y = x * lax.rsqrt(jnp.mean(x * x, axis=-1, keepdims=True) + NORM_EPS)
    return (y * g) * (1.0 + scale) + shift


def _mod_spec(mod, tiles_per_seq):
    rows = mod.shape[1]
    return pl.BlockSpec((1, rows, D), lambda i: (i // tiles_per_seq, 0, 0))


def _full_spec(a):
    nd = a.ndim
    return pl.BlockSpec(a.shape, lambda i: (0,) * nd, pipeline_mode=pl.Buffered(1))


def _ada_body(c_ref, w_ref, b_ref, o_ref):
    c = c_ref[...]
    s = (c * jax.nn.sigmoid(c)).astype(BF16)
    o_ref[0] = _dot(s, w_ref[0].astype(BF16)) + b_ref[0]


def _ada_mod(c_all, w, b):
    layers, rows, tn = w.shape[0], c_all.shape[0], 768
    return pl.pallas_call(
        _ada_body,
        grid=(layers, 3 * D // tn),
        in_specs=[pl.BlockSpec((rows, D), lambda l, j: (0, 0)),
                  pl.BlockSpec((1, D, tn), lambda l, j: (l, 0, j)),
                  pl.BlockSpec((1, 1, tn), lambda l, j: (l, 0, j))],
        out_specs=pl.BlockSpec((1, rows, tn), lambda l, j: (l, 0, j)),
        out_shape=jax.ShapeDtypeStruct((layers, rows, 3 * D), F32),
        compiler_params=_params(("arbitrary", "arbitrary")),
        name="ada_mod",
    )(c_all, w, b.reshape(layers, 1, 3 * D))


def _qkv_body(x_ref, g_ref, sc_ref, sh_ref, wq_ref, wkt_ref, wvt_ref, wk_ref, wv_ref,
              q_ref, kt_ref, vt_ref, ktb_ref, k_ref, v_ref):
    h = _adanorm(x_ref[...], g_ref[...], sc_ref[0], sh_ref[0]).astype(BF16)
    q_ref[...] = _dot(h, wq_ref[...]) * Q_SCALE
    kt = _dot_nt(wkt_ref[...], h)
    kt_ref[...] = kt
    ktb_ref[...] = kt.astype(BF16)
    vt_ref[...] = _dot_nt(wvt_ref[...], h)
    k_ref[...] = _dot(h, wk_ref[...]).astype(k_ref.dtype)
    v_ref[...] = _dot(h, wv_ref[...]).astype(v_ref.dtype)


def _qkv(x, g, scale, shift, w, tm, tiles_per_seq, row_dtype):
    n = x.shape[0]
    seq = tm * tiles_per_seq
    wq, wk, wv = w[:, 0:D], w[:, D:2 * D], w[:, 2 * D:3 * D]
    row = pl.BlockSpec((tm, D), lambda i: (i, 0))
    cm = pl.BlockSpec((None, D, tm), lambda i: (i // tiles_per_seq, 0, i % tiles_per_seq))
    return pl.pallas_call(
        _qkv_body,
        grid=(n // tm,),
        in_specs=[row, _full_spec(g), _mod_spec(scale, tiles_per_seq),
                  _mod_spec(shift, tiles_per_seq)] + [_full_spec(wq)] * 5,
        out_specs=[row, cm, cm, cm, row, row],
        out_shape=[jax.ShapeDtypeStruct((n, D), F32),
                   jax.ShapeDtypeStruct((n // seq, D, seq), F32),
                   jax.ShapeDtypeStruct((n // seq, D, seq), F32),
                   jax.ShapeDtypeStruct((n // seq, D, seq), BF16),
                   jax.ShapeDtypeStruct((n, D), row_dtype),
                   jax.ShapeDtypeStruct((n, D), row_dtype)],
        compiler_params=_params(("arbitrary",)),
        name="qkv",
    )(x, g, scale, shift, wq, wk.T, wv.T, wk, wv)


def _proj_res_body(o_ref, x_ref, gate_ref, w_ref, y_ref):
    y_ref[...] = x_ref[...] + gate_ref[0] * _dot(o_ref[...].astype(BF16), w_ref[...])


def _proj_res(o, x, gate, w, tm, tiles_per_seq):
    n = x.shape[0]
    row = pl.BlockSpec((tm, D), lambda i: (i, 0))
    return pl.pallas_call(
        _proj_res_body,
        grid=(n // tm,),
        in_specs=[row, row, _mod_spec(gate, tiles_per_seq), _full_spec(w)],
        out_specs=row,
        out_shape=jax.ShapeDtypeStruct((n, D), F32),
        compiler_params=_params(("arbitrary",)),
        name="proj_res",
    )(o, x, gate, w)


def _mlp_body(x_ref, g_ref, sc_ref, sh_ref, gate_ref, wu_ref, wd_ref, fin_ref, y_ref,
              *, final_norm):
    x = x_ref[...]
    h = _adanorm(x, g_ref[...], sc_ref[0], sh_ref[0]).astype(BF16)
    acc = jnp.zeros_like(x)
    for c in range(DFF // D):
        u = jnp.maximum(_dot(h, wu_ref[:, c * D:(c + 1) * D]), 0.0)
        acc = acc + _dot((u * u).astype(BF16), wd_ref[c * D:(c + 1) * D, :])
    y = x + gate_ref[0] * acc
    if final_norm:
        y = (y * lax.rsqrt(jnp.mean(y * y, axis=-1, keepdims=True) + NORM_EPS)) * fin_ref[...]
    y_ref[...] = y


def _mlp(x, g, scale, shift, gate, wu, wd, fin, tm, tiles_per_seq, final_norm):
    n = x.shape[0]
    row = pl.BlockSpec((tm, D), lambda i: (i, 0))
    return pl.pallas_call(
        functools.partial(_mlp_body, final_norm=final_norm),
        grid=(n // tm,),
        in_specs=[row, _full_spec(g), _mod_spec(scale, tiles_per_seq),
                  _mod_spec(shift, tiles_per_seq), _mod_spec(gate, tiles_per_seq),
                  _full_spec(wu), _full_spec(wd), _full_spec(fin)],
        out_specs=row,
        out_shape=jax.ShapeDtypeStruct((n, D), F32),
        compiler_params=_params(("arbitrary",)),
        name="mlp",
    )(x, g, scale, shift, gate, wu, wd, fin)


def _norm_body(x_ref, g_ref, sc_ref, sh_ref, h_ref):
    h_ref[...] = _adanorm(x_ref[...], g_ref[...], sc_ref[0], sh_ref[0])


def _norm(x, g, scale, shift, tm, tiles_per_seq):
    n = x.shape[0]
    row = pl.BlockSpec((tm, D), lambda i: (i, 0))
    return pl.pallas_call(
        _norm_body,
        grid=(n // tm,),
        in_specs=[row, _full_spec(g), _mod_spec(scale, tiles_per_seq),
                  _mod_spec(shift, tiles_per_seq)],
        out_specs=row,
        out_shape=jax.ShapeDtypeStruct((n, D), F32),
        compiler_params=_params(("arbitrary",)),
        name="ada_norm",
    )(x, g, scale, shift)


def _suffix_ones():
    j = lax.broadcasted_iota(jnp.int32, (2 * LANES, 2 * LANES), 0)
    s = lax.broadcasted_iota(jnp.int32, (2 * LANES, 2 * LANES), 1)
    return jnp.where(j >= s, 1.0, 0.0).astype(BF16)


def _split(x):
    hi = x.astype(BF16)
    return hi, (x - hi.astype(F32)).astype(BF16)


def _rows(x, y):
    return tuple(jnp.concatenate([a, b], axis=0) for a, b in zip(x, y))


def _dot3(x, y):
    (xh, xl), (yh, yl) = x, y
    return _dot(jnp.concatenate([xh, xh, xl], axis=1), jnp.concatenate([yh, yl, yh], axis=0))


def _dot3_nt(x, y):
    (xh, xl), (yh, yl) = x, y
    return _dot_nt(jnp.concatenate([xh, xh, xl], axis=1), jnp.concatenate([yh, yl, yh], axis=1))


def _sb_weights(z, mask, uo, carry):
    n = z.shape[1] // LANES
    sp = jnp.maximum(z, 0.0) + jnp.log(1.0 + jnp.exp2(-jnp.abs(z))) * LOG2E
    if mask is not None:
        sp = jnp.where(mask, sp, 0.0)
    sp_b = sp.astype(BF16)
    if n == 1:
        cs = _dot(sp_b, uo[LANES:, :])
        a = jnp.exp2(z - cs[:, LANES:] - carry)
        carry = carry + cs[:, :LANES]
    else:
        parts = [None] * (n // 2)
        for s in reversed(range(n // 2)):
            sl = slice(2 * s * LANES, 2 * (s + 1) * LANES)
            cs = _dot(sp_b[:, sl], uo)
            parts[s] = jnp.exp2(z[:, sl] - cs - jnp.concatenate([carry, carry], axis=1))
            carry = carry + jnp.broadcast_to(cs[:, 0:1], carry.shape)
        a = parts[0] if n == 2 else jnp.concatenate(parts, axis=1)
    if mask is not None:
        a = jnp.where(mask, a, 0.0)
    return a, carry


def _sbp_body(bias_ref, q_ref, k_ref, v_ref, uo_ref, o_ref, acc_ref, car_ref, *, tq):
    p = pl.program_id(1)
    i = pl.program_id(2)
    wide = 2 * tq
    lane = lax.broadcasted_iota(jnp.int32, (tq, LANES), 1)
    q = q_ref[...]
    qh = (jnp.where(lane < HD, q, 0.0).astype(BF16), jnp.where(lane >= HD, q, 0.0).astype(BF16))
    acc_ref[...] = jnp.zeros_like(acc_ref)
    car_ref[...] = jnp.zeros_like(car_ref)
    uo = uo_ref[...]

    def block(off, width, masked):
        kt = k_ref[:, pl.ds(off, width)]
        v = v_ref[pl.ds(off, width), :]
        mask = None
        if masked:
            mask = (lax.broadcasted_iota(jnp.int32, (tq, width), 1)
                    < lax.broadcasted_iota(jnp.int32, (tq, width), 0))
        zs = [_dot(qh[hh], kt) + bias_ref[2 * p + hh] * LOG2E for hh in range(2)]
        ws = [_sb_weights(zs[hh], mask, uo, car_ref[hh]) for hh in range(2)]
        for hh in range(2):
            acc_ref[hh] += _dot(ws[hh][0].astype(BF16), v)
            car_ref[hh] = ws[hh][1]

    block(pl.multiple_of(i * tq, tq), tq, True)

    @pl.when(i % 2 == 1)
    def _():
        block(pl.multiple_of((i - 1) * tq, tq), tq, False)

    def body(n, c):
        block(pl.multiple_of((i // 2 - 1 - n) * wide, wide), wide, False)
        return c

    lax.fori_loop(0, i // 2, body, 0)
    o_ref[...] = jnp.where(lane < HD, acc_ref[0], acc_ref[1]).astype(o_ref.dtype)


def _sb_prompt(q, ktb, vb, bias, batch, seq, tq):
    n = q.shape[0]
    nq = seq // tq
    return pl.pallas_call(
        functools.partial(_sbp_body, tq=tq),
        grid=(batch, PAIRS, nq),
        in_specs=[pl.BlockSpec(memory_space=pltpu.SMEM),
                  pl.BlockSpec((tq, LANES), lambda b, p, i: (b * nq + i, p)),
                  pl.BlockSpec((None, LANES, seq), lambda b, p, i: (b, p, 0)),
                  pl.BlockSpec((seq, LANES), lambda b, p, i: (b, p)),
                  pl.BlockSpec((2 * LANES, 2 * LANES), lambda b, p, i: (0, 0))],
        out_specs=pl.BlockSpec((tq, LANES), lambda b, p, i: (b * nq + i, p)),
        out_shape=jax.ShapeDtypeStruct((n, D), BF16),
        scratch_shapes=[pltpu.VMEM((2, tq, LANES), F32), pltpu.VMEM((2, tq, LANES), F32)],
        compiler_params=_params(("arbitrary", "arbitrary", "arbitrary")),
        name="sb_prompt",
    )(bias, q, ktb, vb, _suffix_ones())


def _sbs_body(pt_ref, q_ref, knt_ref, vnt_ref, *rest, n_new, n_pages, group):
    del pt_ref
    kc_refs, vc_refs = rest[:group], rest[group:2 * group]
    bias_ref, uo_ref, o_ref, qb_ref, acc_ref, car_ref = rest[2 * group:]
    b = pl.program_id(0)
    p = pl.program_id(1)
    row = lax.broadcasted_iota(jnp.int32, (LANES, D), 0)
    col = lax.broadcasted_iota(jnp.int32, (LANES, D), 1)
    own = (row // n_new) == (col // HD)
    uo = uo_ref[...]

    def sweep(kts, vts, mask):
        z = jnp.concatenate([_dot(qb_ref[...], kt.astype(BF16)) + bias_ref[...] for kt in kts], axis=1)
        a, car = _sb_weights(z, mask, uo, car_ref[...])
        car_ref[...] = car
        a = a.astype(BF16)
        o = _dot_nt(a[:, 0:LANES], vts[0].astype(BF16))
        for r in range(1, len(vts)):
            o = o + _dot_nt(a[:, r * LANES:(r + 1) * LANES], vts[r].astype(BF16))
        acc_ref[...] += o

    @pl.when(p == 0)
    def _():
        q = q_ref[...]
        qt = jnp.broadcast_to(q[None], (HEADS, n_new, D)).reshape(LANES, D)
        qb_ref[...] = jnp.where(own, qt, 0.0).astype(BF16)
        acc_ref[...] = jnp.zeros_like(acc_ref)
        car_ref[...] = jnp.zeros_like(car_ref)
        off = (b % (LANES // n_new)) * n_new
        r = lax.broadcasted_iota(jnp.int32, (LANES, LANES), 0)
        s = lax.broadcasted_iota(jnp.int32, (LANES, LANES), 1) - off
        sweep([knt_ref[...]], [vnt_ref[...]], (s >= 0) & (s < (r % n_new)))

    sweep([r[...] for r in kc_refs], [r[...] for r in vc_refs], None)

    @pl.when(p == n_pages // group - 1)
    def _():
        o = jnp.where(own, acc_ref[...], 0.0).reshape(HEADS, n_new, D)
        o_ref[...] = jnp.sum(o, axis=0)


def _sb_sample(q, knt, vnt, cache_kt, cache_vt, page_table, bias, n_new, group=4):
    n = q.shape[0]
    batch, n_pages = page_table.shape
    page = cache_kt.shape[2]
    assert page == LANES and HEADS * n_new == LANES and n_pages % group == 0 and n % LANES == 0
    bias_rows = jnp.broadcast_to(jnp.repeat(bias * LOG2E, n_new)[:, None], (LANES, LANES))
    per_tile = LANES // n_new
    new = pl.BlockSpec((n_new, D), lambda b, p, pt: (b, 0))
    newt = pl.BlockSpec((D, LANES), lambda b, p, pt: (0, b // per_tile))

    def cache(r):
        return pl.BlockSpec((None, D, page),
                            lambda b, p, pt: (pt[b, n_pages - group * (p + 1) + r], 0, 0))

    return pl.pallas_call(
        functools.partial(_sbs_body, n_new=n_new, n_pages=n_pages, group=group),
        grid_spec=pltpu.PrefetchScalarGridSpec(
            num_scalar_prefetch=1,
            grid=(batch, n_pages // group),
            in_specs=[new, newt, newt] + [cache(r) for r in range(group)] * 2
            + [pl.BlockSpec((LANES, LANES), lambda b, p, pt: (0, 0)),
               pl.BlockSpec((2 * LANES, 2 * LANES), lambda b, p, pt: (0, 0))],
            out_specs=new,
            scratch_shapes=[pltpu.VMEM((LANES, D), BF16), pltpu.VMEM((LANES, D), F32),
                            pltpu.VMEM((LANES, LANES), F32)]),
        out_shape=jax.ShapeDtypeStruct((n, D), F32),
        compiler_params=_params(("arbitrary", "arbitrary")),
        name="sb_sample",
    )(page_table, q, knt, vnt, *([cache_kt] * group), *([cache_vt] * group), bias_rows, _suffix_ones())


def _lanes(col, tm):
    return jnp.concatenate([col] * (tm // LANES), axis=1)


def _rw_proj_body(h_ref, prev_ref, mu_ref, wr_ref, wk_ref, wv_ref, w1_ref, w2_ref, a1_ref,
                  a2_ref, g1_ref, g2_ref, w0_ref, a0_ref, kk_ref, ka_ref,
                  r_out, d_out, lw_out, k_out, kkn_out, b_out, v_out, g_out, *, tm):
    h = h_ref[...]
    xx = prev_ref[...] - h
    xr, xw, xk, xv, xa, xg = ((h + xx * mu_ref[n:n + 1, :]).astype(BF16) for n in range(6))
    r_out[...] = _dot_nt(wr_ref[...], xr)
    k = _dot_nt(wk_ref[...], xk)
    v_out[...] = _dot_nt(wv_ref[...], xv)
    w_lin = _lanes(w0_ref[...], tm) + _dot_nt(w2_ref[...], jnp.tanh(_dot(xw, w1_ref[...])).astype(BF16))
    w = -_softplus(-w_lin) - 0.5
    log_decay = -jnp.exp(w)
    lw_out[...] = log_decay
    d_out[...] = jnp.exp(log_decay)
    a = jax.nn.sigmoid(_lanes(a0_ref[...], tm) + _dot_nt(a2_ref[...], _dot(xa, a1_ref[...]).astype(BF16)))
    g_out[...] = _dot_nt(g2_ref[...], jax.nn.sigmoid(_dot(xg, g1_ref[...])).astype(BF16))
    kk = (k * _lanes(kk_ref[...], tm)).reshape(HEADS, HD, tm)
    norm = jnp.sqrt(jnp.sum(kk * kk, axis=1, keepdims=True))
    kk = (kk / jnp.maximum(norm, 1e-12)).reshape(D, tm)
    kkn_out[...] = kk
    b_out[...] = kk * a
    k_out[...] = k * (1.0 + (a - 1.0) * _lanes(ka_ref[...], tm))


def _rw_proj(h, prev, mats, cols, tm):
    n = h.shape[0]
    row = pl.BlockSpec((tm, D), lambda i: (i, 0))
    out = pl.BlockSpec((D, tm), lambda i: (0, i))
    return pl.pallas_call(
        functools.partial(_rw_proj_body, tm=tm),
        grid=(n // tm,),
        in_specs=[row, row] + [_full_spec(m) for m in mats] + [_full_spec(c) for c in cols],
        out_specs=[out] * 8,
        out_shape=[jax.ShapeDtypeStruct((D, n), F32)] * 8,
        compiler_params=_params(("arbitrary",)),
        name="rw_proj",
    )(h, prev, *mats, *cols)


def _chunk_body(r_ref, lw_ref, k_ref, kk_ref, b_ref, v_ref, y_ref, sT_ref, sbd_ref, ybuf_ref,
                *, n_chunks):
    c = pl.program_id(2)

    @pl.when(c == 0)
    def _():
        sbd_ref[...] = jnp.zeros_like(sbd_ref)

    m_a = lax.broadcasted_iota(jnp.int32, (CHUNK, LANES), 1) < HD
    rr = lax.broadcasted_iota(jnp.int32, (LANES, LANES), 0)
    cc = lax.broadcasted_iota(jnp.int32, (LANES, LANES), 1)
    same = (rr // CHUNK) == (cc // CHUNK)
    strict = same & (cc < rr)
    incl = same & (cc <= rr)
    ident = rr == cc
    tr = lax.broadcasted_iota(jnp.int32, (CHUNK, CHUNK), 0)
    tc = lax.broadcasted_iota(jnp.int32, (CHUNK, CHUNK), 1)
    tri = jnp.where(tc <= tr, 1.0, 0.0).astype(BF16)

    def stack(x):
        return jnp.concatenate([jnp.where(m_a, x, 0.0), jnp.where(m_a, 0.0, x)], axis=0)

    tiles = [ref[...].T for ref in (r_ref, lw_ref, k_ref, kk_ref, b_ref, v_ref)]

    ch = range(n_chunks)
    sls = [slice(n * CHUNK, (n + 1) * CHUNK) for n in ch]
    r, lw, k, kk, b, v = ([x[sl] for sl in sls] for x in tiles)
    lw_parts = [_split(x) for x in lw]
    cum = [_dot(tri, hi) + _dot(tri, lo) for hi, lo in lw_parts]
    cum_end = [x[CHUNK - 1:CHUNK, :] for x in cum]
    e_neg = [jnp.exp(-x) for x in cum]
    e_end = [jnp.exp(ce - x) for ce, x in zip(cum_end, cum)]
    kap = [_split(stack(kk[n] * jnp.exp(cum[n] - lw[n]))) for n in ch]
    rti = [_split(stack(r[n] * jnp.exp(cum[n]))) for n in ch]
    v_s = [_split(stack(x)) for x in v]
    bet = [_split(stack(b[n] * e_neg[n])) for n in ch]
    kti = [_split(stack(k[n] * e_neg[n])) for n in ch]
    gram = [_dot3_nt(_rows(kap[n], rti[n]), _rows(bet[n], kti[n])) for n in ch]
    a_k = [_split(jnp.where(strict, x[0:LANES, LANES:], 0.0)) for x in gram]
    p_k = [_split(jnp.where(incl, x[LANES:, LANES:], 0.0)) for x in gram]
    p_b = [_split(jnp.where(incl, -x[LANES:, 0:LANES], 0.0)) for x in gram]

    a_b = [jnp.where(strict, x[0:LANES, 0:LANES], 0.0) for x in gram]
    t_inv = [jnp.where(ident, 1.0, jnp.where((rr // 2) == (cc // 2), -x, 0.0)) for x in a_b]
    size = 2
    while size < CHUNK:
        join = ((rr // (2 * size)) == (cc // (2 * size))) & ((rr // size) != (cc // size))
        t_parts = [_split(x) for x in t_inv]
        tb = [_dot3(t_parts[n], _split(jnp.where(join, a_b[n], 0.0))) for n in ch]
        t_inv = [t_inv[n] - _dot3(_split(tb[n]), t_parts[n]) for n in ch]
        size *= 2
    t_low = [_split(jnp.where(ident, 0.0, x)) for x in t_inv]

    kap_f = [stack(kk[n] * jnp.exp(cum[n] - lw[n])) for n in ch]
    w = [_split(kap_f[n] + _dot3(t_low[n], kap[n])) for n in ch]
    av = [_dot3(a_k[n], v_s[n]) for n in ch]
    u = [av[n] + _dot3(t_low[n], _split(av[n])) for n in ch]
    u_s = [_split(x) for x in u]
    khat_t = [_split(stack(k[n] * e_end[n]).T) for n in ch]
    bhat_t = [_split(stack(b[n] * e_end[n]).T) for n in ch]
    m = [_split(_dot3(bhat_t[n], w[n])) for n in ch]
    const = [_dot3(khat_t[n], v_s[n]) - _dot3(bhat_t[n], u_s[n]) for n in ch]
    g_col = [jnp.broadcast_to(jnp.exp(x), (LANES, LANES)).T for x in cum_end]

    sbd = sbd_ref[...]
    for n in ch:
        s_s = _split(sbd)
        d_s = _split(_dot3(w[n], s_s) + u[n])
        y_s = _dot3(rti[n], s_s) + _dot3(p_k[n], v_s[n]) + _dot3(p_b[n], d_s)
        ybuf_ref[sls[n], :] = y_s[0:CHUNK] + y_s[CHUNK:]
        sbd = g_col[n] * sbd - _dot3(m[n], s_s) + const[n]
    sbd_ref[...] = sbd

    y_ref[...] = ybuf_ref[...].T

    @pl.when(c == pl.num_programs(2) - 1)
    def _():
        s_t = sbd.T
        top = lax.broadcasted_iota(jnp.int32, (LANES, HD), 0) < HD
        sT_ref[...] = jnp.where(top, s_t[:, 0:HD], s_t[:, HD:])


def _rw_chunk_prompt(r, lw, k, kk, b, v, batch, seq, n_chunks=8):
    width = n_chunks * CHUNK
    assert CHUNK == HD and seq % width == 0 and width % LANES == 0
    nc = seq // width
    tile = pl.BlockSpec((LANES, width), lambda p, bb, c: (p, bb * nc + c))
    st = pl.BlockSpec((None, None, LANES, HD), lambda p, bb, c: (bb, p, 0, 0))
    return pl.pallas_call(
        functools.partial(_chunk_body, n_chunks=n_chunks),
        grid=(PAIRS, batch, nc),
        in_specs=[tile] * 6,
        out_specs=[tile, st],
        out_shape=[jax.ShapeDtypeStruct(r.shape, F32),
                   jax.ShapeDtypeStruct((batch, PAIRS, LANES, HD), F32)],
        scratch_shapes=[pltpu.VMEM((LANES, LANES), F32), pltpu.VMEM((width, LANES), F32)],
        compiler_params=_params(("arbitrary", "arbitrary", "arbitrary")),
        name="rw_chunk_prompt",
    )(r, lw, k, kk, b, v)


def _scan_body(r_ref, d_ref, k_ref, kk_ref, b_ref, v_ref, s0_ref, y_ref, sT_ref,
               vbuf_ref, ybuf_ref, *, n_groups):
    first = lax.broadcasted_iota(jnp.int32, (HD, LANES), 1) < HD
    vbuf_ref[...] = v_ref[...].T

    def group(g, carry):
        base = pl.multiple_of((g // (LANES // GROUP)) * LANES, LANES)
        gl = g % (LANES // GROUP)
        shift = jnp.where(gl == 0, 0, LANES - gl * GROUP)
        tiles = [pltpu.roll(ref[:, pl.ds(base, LANES)], shift, 1)
                 for ref in (d_ref, kk_ref, b_ref, k_ref, r_ref)]
        s_t = s0_ref[g].T
        for s in range(GROUP):
            dc, kkc, bc, kc, rc = (jnp.where(first, x[0:HD, s:s + 1], x[HD:2 * HD, s:s + 1])
                                   for x in tiles)
            t = g * GROUP + s
            sa = jnp.sum(s_t * kkc, axis=0, keepdims=True)
            s_t = s_t * dc - bc * sa + kc * vbuf_ref[pl.ds(t, 1), :]
            ybuf_ref[pl.ds(t, 1), :] = jnp.sum(s_t * rc, axis=0, keepdims=True)
        sT_ref[g] = s_t.T
        return carry

    lax.fori_loop(0, n_groups, group, 0)
    y_ref[...] = ybuf_ref[...].T


def _rw_scan_sample(r, d, k, kk, b, v, s0, batch, seq):
    assert seq == GROUP
    n = batch * seq
    tile = pl.BlockSpec((LANES, n), lambda p: (p, 0))
    st = pl.BlockSpec((batch, None, LANES, HD), lambda p: (0, p, 0, 0))
    return pl.pallas_call(
        functools.partial(_scan_body, n_groups=batch),
        grid=(PAIRS,),
        in_specs=[tile] * 6 + [st],
        out_specs=[tile, st],
        out_shape=[jax.ShapeDtypeStruct(r.shape, F32),
                   jax.ShapeDtypeStruct((batch, PAIRS, LANES, HD), F32)],
        scratch_shapes=[pltpu.VMEM((n, LANES), F32), pltpu.VMEM((n, LANES), F32)],
        compiler_params=_params(("arbitrary",)),
        name="rw_scan_sample",
    )(r, d, k, kk, b, v, s0)


def _rw_out_body(y_ref, r_ref, k_ref, v_ref, g_ref, x_ref, gate_ref, wo_ref, lng_ref, lnb_ref,
                 rk_ref, o_ref, *, tm):
    y = y_ref[...].reshape(HEADS, HD, tm)
    mean = jnp.mean(y, axis=1, keepdims=True)
    var = jnp.mean(jnp.square(y - mean), axis=1, keepdims=True)
    y = ((y - mean) * lax.rsqrt(var + GN_EPS)).reshape(D, tm) * _lanes(lng_ref[...], tm) \
        + _lanes(lnb_ref[...], tm)
    rk = (r_ref[...] * k_ref[...] * _lanes(rk_ref[...], tm)).reshape(HEADS, HD, tm)
    bonus = jnp.sum(rk, axis=1, keepdims=True) * v_ref[...].reshape(HEADS, HD, tm)
    yg = (y + bonus.reshape(D, tm)) * g_ref[...]
    o = _dot(yg.T.astype(BF16), wo_ref[...])
    o_ref[...] = x_ref[...] + gate_ref[0] * o


def _rw_out(y, r, k, v, g, x, gate, wo, cols, tm, tiles_per_seq):
    n = x.shape[0]
    row = pl.BlockSpec((tm, D), lambda i: (i, 0))
    cm = pl.BlockSpec((D, tm), lambda i: (0, i))
    return pl.pallas_call(
        functools.partial(_rw_out_body, tm=tm),
        grid=(n // tm,),
        in_specs=[cm] * 5 + [row, _mod_spec(gate, tiles_per_seq), _full_spec(wo)]
        + [_full_spec(c) for c in cols],
        out_specs=row,
        out_shape=jax.ShapeDtypeStruct((n, D), F32),
        compiler_params=_params(("arbitrary",)),
        name="rw_out",
    )(y, r, k, v, g, x, gate, wo, *cols)


def _col(p):
    return jnp.broadcast_to(p.reshape(D, 1), (D, LANES))


def kernel(x_prompt, x_sample, cache_k, cache_v, state_wkv, state_shift, page_table, c_prompt, c_sample, norm_mix, ada_w_mix, ada_b_mix, norm_ffn, ada_w_ffn, ada_b_ffn, w_up, w_down, sb_w_qkv, sb_w_o, sb_bias, rw_mu, rw_w_r, rw_w_k, rw_w_v, rw_w_o, rw_w0, rw_w1, rw_w2, rw_a0, rw_a1, rw_a2, rw_g1, rw_g2, rw_k_k, rw_k_a, rw_r_k, rw_ln_g, rw_ln_b, final_norm):
    batch, seq, _ = x_prompt.shape
    dbatch, dseq, _ = x_sample.shape
    depth = norm_mix.shape[0]
    n_p, n_s = batch * seq, dbatch * dseq
    tm_p, tm_s = 512, n_s
    tps_p, tps_s = seq // tm_p, 1

    mod_mix = _ada_mod(jnp.concatenate([c_prompt, c_sample], axis=0), ada_w_mix, ada_b_mix)
    mod_ffn = _ada_mod(jnp.concatenate([c_prompt, c_sample], axis=0), ada_w_ffn, ada_b_ffn)

    def split(mod):
        parts = [mod[:, n * D:(n + 1) * D] for n in range(3)]
        return ([m[:batch].reshape(batch, 1, D) for m in parts],
                [jnp.repeat(m[batch:], dseq, axis=0).reshape(1, n_s, D) for m in parts])

    xp = x_prompt.reshape(n_p, D)
    xs = x_sample.reshape(n_s, D)
    fin = final_norm.reshape(1, D)
    outs = {}

    for i in range(depth):
        j = i // 2
        g_mix = norm_mix[i].reshape(1, D)
        (sh_p, sc_p, ga_p), (sh_s, sc_s, ga_s) = split(mod_mix[i])
        if i % 2 == 0:
            w_qkv = sb_w_qkv[j].astype(BF16)
            w_o = sb_w_o[j].astype(BF16)
            qp, ktp, vtp, ktbp, _, vbp = _qkv(xp, g_mix, sc_p, sh_p, w_qkv, tm_p, tps_p, BF16)
            qs, kts, vts, _, ks, vs = _qkv(xs, g_mix, sc_s, sh_s, w_qkv, tm_s, tps_s, F32)
            op = _sb_prompt(qp, ktbp, vbp, sb_bias[j], batch, seq, 256)
            n_phys, page = cache_k.shape[1], cache_k.shape[2]
            cache_kt = jnp.transpose(cache_k[j], (0, 2, 3, 1)).reshape(n_phys, D, page)
            cache_vt = jnp.transpose(cache_v[j], (0, 2, 3, 1)).reshape(n_phys, D, page)
            os_ = _sb_sample(qs, kts[0], vts[0], cache_kt, cache_vt, page_table, sb_bias[j], dseq)
            xp = _proj_res(op, xp, ga_p, w_o, tm_p, tps_p)
            xs = _proj_res(os_, xs, ga_s, w_o, tm_s, tps_s)
            outs.setdefault("kp", []).append(
                jnp.transpose(ktp.reshape(batch, HEADS, HD, seq), (0, 3, 1, 2)))
            outs.setdefault("vp", []).append(
                jnp.transpose(vtp.reshape(batch, HEADS, HD, seq), (0, 3, 1, 2)))
            outs.setdefault("ks", []).append(ks.reshape(dbatch, dseq, HEADS, HD))
            outs.setdefault("vs", []).append(vs.reshape(dbatch, dseq, HEADS, HD))
        else:
            pad = GATE_LORA_PAD - rw_g1.shape[-1]
            mats = (jnp.pad(rw_mu[j], ((0, 2), (0, 0))),
                    rw_w_r[j].T.astype(BF16), rw_w_k[j].T.astype(BF16), rw_w_v[j].T.astype(BF16),
                    rw_w1[j].astype(BF16), rw_w2[j].T.astype(BF16),
                    rw_a1[j].astype(BF16), rw_a2[j].T.astype(BF16),
                    jnp.pad(rw_g1[j], ((0, 0), (0, pad))).astype(BF16),
                    jnp.pad(rw_g2[j], ((0, pad), (0, 0))).T.astype(BF16))
            cols = (_col(rw_w0[j]), _col(rw_a0[j]), _col(rw_k_k[j]), _col(rw_k_a[j]))
            ocols = (_col(rw_ln_g[j]), _col(rw_ln_b[j]), _col(rw_r_k[j]))
            w_o = rw_w_o[j].astype(BF16)

            hp = _norm(xp, g_mix, sc_p, sh_p, tm_p, tps_p).reshape(batch, seq, D)
            hs = _norm(xs, g_mix, sc_s, sh_s, tm_s, tps_s).reshape(dbatch, dseq, D)
            prev_p = jnp.concatenate([jnp.zeros((batch, 1, D), F32), hp[:, :-1]], axis=1)
            prev_s = jnp.concatenate([state_shift[j][:, None, :], hs[:, :-1]], axis=1)

            rp, _, lwp, kp_, kkp, bp, vp_, gp = _rw_proj(hp.reshape(n_p, D), prev_p.reshape(n_p, D),
                                                         mats, cols, 256)
            rs, ds, _, ks_, kks, bs, vs_, gs = _rw_proj(hs.reshape(n_s, D), prev_s.reshape(n_s, D),
                                                        mats, cols, n_s)
            yp, wkv_p = _rw_chunk_prompt(rp, lwp, kp_, kkp, bp, vp_, batch, seq)
            s0 = state_wkv[j].reshape(dbatch, PAIRS, LANES, HD)
            ys, wkv_s = _rw_scan_sample(rs, ds, ks_, kks, bs, vs_, s0, dbatch, dseq)
            xp = _rw_out(yp, rp, kp_, vp_, gp, xp, ga_p, w_o, ocols, 256, seq // 256)
            xs = _rw_out(ys, rs, ks_, vs_, gs, xs, ga_s, w_o, ocols, n_s, 1)
            outs.setdefault("wp", []).append(wkv_p.reshape(batch, HEADS, HD, HD))
            outs.setdefault("sp", []).append(hp[:, -1])
            outs.setdefault("ws", []).append(wkv_s.reshape(dbatch, HEADS, HD, HD))
            outs.setdefault("ss", []).append(hs[:, -1])

        g_ffn = norm_ffn[i].reshape(1, D)
        (sh_p, sc_p, ga_p), (sh_s, sc_s, ga_s) = split(mod_ffn[i])
        wu = w_up[i].astype(BF16)
        wd = w_down[i].astype(BF16)
        last = i == depth - 1
        xp = _mlp(xp, g_ffn, sc_p, sh_p, ga_p, wu, wd, fin, tm_p, tps_p, last)
        xs = _mlp(xs, g_ffn, sc_s, sh_s, ga_s, wu, wd, fin, tm_s, tps_s, last)

    return (xp.reshape(batch, seq, D), xs.reshape(dbatch, dseq, D),
            jnp.stack(outs["kp"]), jnp.stack(outs["vp"]), jnp.stack(outs["ks"]),
            jnp.stack(outs["vs"]), jnp.stack(outs["wp"]), jnp.stack(outs["sp"]),
            jnp.stack(outs["ws"]), jnp.stack(outs["ss"]))
```

```python
import functools

import jax
import jax.numpy as jnp
from jax import lax
from jax.experimental import pallas as pl
from jax.experimental.pallas import tpu as pltpu

F32 = jnp.float32
BF16 = jnp.bfloat16

D = 1024
HEADS = 16
HD = 64
PAIRS = HEADS // 2
LANES = 128
DFF = 4 * D
NORM_EPS = 1e-6
GN_EPS = 1e-5 * HD
LOG2E = 1.4426950408889634
Q_SCALE = HD ** -0.5 * LOG2E
GATE_LORA_PAD = 256
GROUP = 8
CHUNK = 64
VMEM_LIMIT = 56 * 1024 * 1024

NT = (((1,), (1,)), ((), ()))


def _params(sem):
    return pltpu.CompilerParams(dimension_semantics=sem, vmem_limit_bytes=VMEM_LIMIT)


def _dot(a, b):
    return jnp.dot(a, b, preferred_element_type=F32)


def _dot_nt(a, b):
    return lax.dot_general(a, b, NT, preferred_element_type=F32)


def _softplus(x):
    return jnp.maximum(x, 0.0) + jnp.log(1.0 + jnp.exp(-jnp.abs(x)))


def _adanorm(x, g, scale, shift):
    y = x * lax.rsqrt(jnp.mean(x * x, axis=-1, keepdims=True) + NORM_EPS)
    return (y * g) * (1.0 + scale) + shift


def _mod_spec(mod, tiles_per_seq):
    rows = mod.shape[1]
    return pl.BlockSpec((1, rows, D), lambda i: (i // tiles_per_seq, 0, 0))


def _full_spec(a):
    nd = a.ndim
    return pl.BlockSpec(a.shape, lambda i: (0,) * nd, pipeline_mode=pl.Buffered(1))


def _ada_body(c_ref, w_ref, b_ref, o_ref):
    c = c_ref[...]
    s = (c * jax.nn.sigmoid(c)).astype(BF16)
    o_ref[0] = _dot(s, w_ref[0].astype(BF16)) + b_ref[0]


def _ada_mod(c_all, w, b):
    layers, rows, tn = w.shape[0], c_all.shape[0], 768
    return pl.pallas_call(
        _ada_body,
        grid=(layers, 3 * D // tn),
        in_specs=[pl.BlockSpec((rows, D), lambda l, j: (0, 0)),
                  pl.BlockSpec((1, D, tn), lambda l, j: (l, 0, j)),
                  pl.BlockSpec((1, 1, tn), lambda l, j: (l, 0, j))],
        out_specs=pl.BlockSpec((1, rows, tn), lambda l, j: (l, 0, j)),
        out_shape=jax.ShapeDtypeStruct((layers, rows, 3 * D), F32),
        compiler_params=_params(("arbitrary", "arbitrary")),
        name="ada_mod",
    )(c_all, w, b.reshape(layers, 1, 3 * D))


def _qkv_body(x_ref, g_ref, sc_ref, sh_ref, wq_ref, wkt_ref, wvt_ref, wk_ref, wv_ref,
              q_ref, kt_ref, vt_ref, ktb_ref, k_ref, v_ref):
    h = _adanorm(x_ref[...], g_ref[...], sc_ref[0], sh_ref[0]).astype(BF16)
    q_ref[...] = _dot(h, wq_ref[...]) * Q_SCALE
    kt = _dot_nt(wkt_ref[...], h)
    kt_ref[...] = kt
    ktb_ref[...] = kt.astype(BF16)
    vt_ref[...] = _dot_nt(wvt_ref[...], h)
    k_ref[...] = _dot(h, wk_ref[...]).astype(k_ref.dtype)
    v_ref[...] = _dot(h, wv_ref[...]).astype(v_ref.dtype)


def _qkv(x, g, scale, shift, w, tm, tiles_per_seq, row_dtype):
    n = x.shape[0]
    seq = tm * tiles_per_seq
    wq, wk, wv = w[:, 0:D], w[:, D:2 * D], w[:, 2 * D:3 * D]
    row = pl.BlockSpec((tm, D), lambda i: (i, 0))
    cm = pl.BlockSpec((None, D, tm), lambda i: (i // tiles_per_seq, 0, i % tiles_per_seq))
    return pl.pallas_call(
        _qkv_body,
        grid=(n // tm,),
        in_specs=[row, _full_spec(g), _mod_spec(scale, tiles_per_seq),
                  _mod_spec(shift, tiles_per_seq)] + [_full_spec(wq)] * 5,
        out_specs=[row, cm, cm, cm, row, row],
        out_shape=[jax.ShapeDtypeStruct((n, D), F32),
                   jax.ShapeDtypeStruct((n // seq, D, seq), F32),
                   jax.ShapeDtypeStruct((n // seq, D, seq), F32),
                   jax.ShapeDtypeStruct((n // seq, D, seq), BF16),
                   jax.ShapeDtypeStruct((n, D), row_dtype),
                   jax.ShapeDtypeStruct((n, D), row_dtype)],
        compiler_params=_params(("arbitrary",)),
        name="qkv",
    )(x, g, scale, shift, wq, wk.T, wv.T, wk, wv)


def _proj_res_body(o_ref, x_ref, gate_ref, w_ref, y_ref):
    y_ref[...] = x_ref[...] + gate_ref[0] * _dot(o_ref[...].astype(BF16), w_ref[...])


def _proj_res(o, x, gate, w, tm, tiles_per_seq):
    n = x.shape[0]
    row = pl.BlockSpec((tm, D), lambda i: (i, 0))
    return pl.pallas_call(
        _proj_res_body,
        grid=(n // tm,),
        in_specs=[row, row, _mod_spec(gate, tiles_per_seq), _full_spec(w)],
        out_specs=row,
        out_shape=jax.ShapeDtypeStruct((n, D), F32),
        compiler_params=_params(("arbitrary",)),
        name="proj_res",
    )(o, x, gate, w)


def _mlp_body(x_ref, g_ref, sc_ref, sh_ref, gate_ref, wu_ref, wd_ref, fin_ref, y_ref,
              *, final_norm):
    x = x_ref[...]
    h = _adanorm(x, g_ref[...], sc_ref[0], sh_ref[0]).astype(BF16)
    acc = jnp.zeros_like(x)
    for c in range(DFF // D):
        u = jnp.maximum(_dot(h, wu_ref[:, c * D:(c + 1) * D]), 0.0)
        acc = acc + _dot((u * u).astype(BF16), wd_ref[c * D:(c + 1) * D, :])
    y = x + gate_ref[0] * acc
    if final_norm:
        y = (y * lax.rsqrt(jnp.mean(y * y, axis=-1, keepdims=True) + NORM_EPS)) * fin_ref[...]
    y_ref[...] = y


def _mlp(x, g, scale, shift, gate, wu, wd, fin, tm, tiles_per_seq, final_norm):
    n = x.shape[0]
    row = pl.BlockSpec((tm, D), lambda i: (i, 0))
    return pl.pallas_call(
        functools.partial(_mlp_body, final_norm=final_norm),
        grid=(n // tm,),
        in_specs=[row, _full_spec(g), _mod_spec(scale, tiles_per_seq),
                  _mod_spec(shift, tiles_per_seq), _mod_spec(gate, tiles_per_seq),
                  _full_spec(wu), _full_spec(wd), _full_spec(fin)],
        out_specs=row,
        out_shape=jax.ShapeDtypeStruct((n, D), F32),
        compiler_params=_params(("arbitrary",)),
        name="mlp",
    )(x, g, scale, shift, gate, wu, wd, fin)


def _norm_body(x_ref, g_ref, sc_ref, sh_ref, h_ref):
    h_ref[...] = _adanorm(x_ref[...], g_ref[...], sc_ref[0], sh_ref[0])


def _norm(x, g, scale, shift, tm, tiles_per_seq):
    n = x.shape[0]
    row = pl.BlockSpec((tm, D), lambda i: (i, 0))
    return pl.pallas_call(
        _norm_body,
        grid=(n // tm,),
        in_specs=[row, _full_spec(g), _mod_spec(scale, tiles_per_seq),
                  _mod_spec(shift, tiles_per_seq)],
        out_specs=row,
        out_shape=jax.ShapeDtypeStruct((n, D), F32),
        compiler_params=_params(("arbitrary",)),
        name="ada_norm",
    )(x, g, scale, shift)


def _suffix_ones():
    j = lax.broadcasted_iota(jnp.int32, (2 * LANES, 2 * LANES), 0)
    s = lax.broadcasted_iota(jnp.int32, (2 * LANES, 2 * LANES), 1)
    return jnp.where(j >= s, 1.0, 0.0).astype(BF16)


def _split(x):
    hi = x.astype(BF16)
    return hi, (x - hi.astype(F32)).astype(BF16)


def _rows(x, y):
    return tuple(jnp.concatenate([a, b], axis=0) for a, b in zip(x, y))


def _dot3(x, y):
    (xh, xl), (yh, yl) = x, y
    return _dot(jnp.concatenate([xh, xh, xl], axis=1), jnp.concatenate([yh, yl, yh], axis=0))


def _dot3_nt(x, y):
    (xh, xl), (yh, yl) = x, y
    return _dot_nt(jnp.concatenate([xh, xh, xl], axis=1), jnp.concatenate([yh, yl, yh], axis=1))


def _sb_weights(z, mask, uo, carry):
    n = z.shape[1] // LANES
    sp = jnp.maximum(z, 0.0) + jnp.log(1.0 + jnp.exp2(-jnp.abs(z))) * LOG2E
    if mask is not None:
        sp = jnp.where(mask, sp, 0.0)
    sp_b = sp.astype(BF16)
    if n == 1:
        cs = _dot(sp_b, uo[LANES:, :])
        a = jnp.exp2(z - cs[:, LANES:] - carry)
        carry = carry + cs[:, :LANES]
    else:
        parts = [None] * (n // 2)
        for s in reversed(range(n // 2)):
            sl = slice(2 * s * LANES, 2 * (s + 1) * LANES)
            cs = _dot(sp_b[:, sl], uo)
            parts[s] = jnp.exp2(z[:, sl] - cs - jnp.concatenate([carry, carry], axis=1))
            carry = carry + jnp.broadcast_to(cs[:, 0:1], carry.shape)
        a = parts[0] if n == 2 else jnp.concatenate(parts, axis=1)
    if mask is not None:
        a = jnp.where(mask, a, 0.0)
    return a, carry


def _sbp_body(bias_ref, q_ref, k_ref, v_ref, uo_ref, o_ref, acc_ref, car_ref, *, tq):
    p = pl.program_id(1)
    i = pl.program_id(2)
    wide = 2 * tq
    lane = lax.broadcasted_iota(jnp.int32, (tq, LANES), 1)
    q = q_ref[...]
    qh = (jnp.where(lane < HD, q, 0.0).astype(BF16), jnp.where(lane >= HD, q, 0.0).astype(BF16))
    acc_ref[...] = jnp.zeros_like(acc_ref)
    car_ref[...] = jnp.zeros_like(car_ref)
    uo = uo_ref[...]

    def block(off, width, masked):
        kt = k_ref[:, pl.ds(off, width)]
        v = v_ref[pl.ds(off, width), :]
        mask = None
        if masked:
            mask = (lax.broadcasted_iota(jnp.int32, (tq, width), 1)
                    < lax.broadcasted_iota(jnp.int32, (tq, width), 0))
        zs = [_dot(qh[hh], kt) + bias_ref[2 * p + hh] * LOG2E for hh in range(2)]
        ws = [_sb_weights(zs[hh], mask, uo, car_ref[hh]) for hh in range(2)]
        for hh in range(2):
            acc_ref[hh] += _dot(ws[hh][0].astype(BF16), v)
            car_ref[hh] = ws[hh][1]

    block(pl.multiple_of(i * tq, tq), tq, True)

    @pl.when(i % 2 == 1)
    def _():
        block(pl.multiple_of((i - 1) * tq, tq), tq, False)

    def body(n, c):
        block(pl.multiple_of((i // 2 - 1 - n) * wide, wide), wide, False)
        return c

    lax.fori_loop(0, i // 2, body, 0)
    o_ref[...] = jnp.where(lane < HD, acc_ref[0], acc_ref[1]).astype(o_ref.dtype)


def _sb_prompt(q, ktb, vb, bias, batch, seq, tq):
    n = q.shape[0]
    nq = seq // tq
    return pl.pallas_call(
        functools.partial(_sbp_body, tq=tq),
        grid=(batch, PAIRS, nq),
        in_specs=[pl.BlockSpec(memory_space=pltpu.SMEM),
                  pl.BlockSpec((tq, LANES), lambda b, p, i: (b * nq + i, p)),
                  pl.BlockSpec((None, LANES, seq), lambda b, p, i: (b, p, 0)),
                  pl.BlockSpec((seq, LANES), lambda b, p, i: (b, p)),
                  pl.BlockSpec((2 * LANES, 2 * LANES), lambda b, p, i: (0, 0))],
        out_specs=pl.BlockSpec((tq, LANES), lambda b, p, i: (b * nq + i, p)),
        out_shape=jax.ShapeDtypeStruct((n, D), BF16),
        scratch_shapes=[pltpu.VMEM((2, tq, LANES), F32), pltpu.VMEM((2, tq, LANES), F32)],
        compiler_params=_params(("arbitrary", "arbitrary", "arbitrary")),
        name="sb_prompt",
    )(bias, q, ktb, vb, _suffix_ones())


def _sbs_body(pt_ref, q_ref, knt_ref, vnt_ref, *rest, n_new, n_pages, group):
    del pt_ref
    kc_refs, vc_refs = rest[:group], rest[group:2 * group]
    bias_ref, uo_ref, o_ref, qb_ref, acc_ref, car_ref = rest[2 * group:]
    b = pl.program_id(0)
    p = pl.program_id(1)
    row = lax.broadcasted_iota(jnp.int32, (LANES, D), 0)
    col = lax.broadcasted_iota(jnp.int32, (LANES, D), 1)
    own = (row // n_new) == (col // HD)
    uo = uo_ref[...]

    def sweep(kts, vts, mask):
        z = jnp.concatenate([_dot(qb_ref[...], kt.astype(BF16)) + bias_ref[...] for kt in kts], axis=1)
        a, car = _sb_weights(z, mask, uo, car_ref[...])
        car_ref[...] = car
        a = a.astype(BF16)
        o = _dot_nt(a[:, 0:LANES], vts[0].astype(BF16))
        for r in range(1, len(vts)):
            o = o + _dot_nt(a[:, r * LANES:(r + 1) * LANES], vts[r].astype(BF16))
        acc_ref[...] += o

    @pl.when(p == 0)
    def _():
        q = q_ref[...]
        qt = jnp.broadcast_to(q[None], (HEADS, n_new, D)).reshape(LANES, D)
        qb_ref[...] = jnp.where(own, qt, 0.0).astype(BF16)
        acc_ref[...] = jnp.zeros_like(acc_ref)
        car_ref[...] = jnp.zeros_like(car_ref)
        off = (b % (LANES // n_new)) * n_new
        r = lax.broadcasted_iota(jnp.int32, (LANES, LANES), 0)
        s = lax.broadcasted_iota(jnp.int32, (LANES, LANES), 1) - off
        sweep([knt_ref[...]], [vnt_ref[...]], (s >= 0) & (s < (r % n_new)))

    sweep([r[...] for r in kc_refs], [r[...] for r in vc_refs], None)

    @pl.when(p == n_pages // group - 1)
    def _():
        o = jnp.where(own, acc_ref[...], 0.0).reshape(HEADS, n_new, D)
        o_ref[...] = jnp.sum(o, axis=0)


def _sb_sample(q, knt, vnt, cache_kt, cache_vt, page_table, bias, n_new, group=8):
    n = q.shape[0]
    batch, n_pages = page_table.shape
    page = cache_kt.shape[2]
    assert page == LANES and HEADS * n_new == LANES and n_pages % group == 0 and n % LANES == 0
    bias_rows = jnp.broadcast_to(jnp.repeat(bias * LOG2E, n_new)[:, None], (LANES, LANES))
    per_tile = LANES // n_new
    new = pl.BlockSpec((n_new, D), lambda b, p, pt: (b, 0))
    newt = pl.BlockSpec((D, LANES), lambda b, p, pt: (0, b // per_tile))

    def cache(r):
        return pl.BlockSpec((None, D, page),
                            lambda b, p, pt: (pt[b, n_pages - group * (p + 1) + r], 0, 0))

    return pl.pallas_call(
        functools.partial(_sbs_body, n_new=n_new, n_pages=n_pages, group=group),
        grid_spec=pltpu.PrefetchScalarGridSpec(
            num_scalar_prefetch=1,
            grid=(batch, n_pages // group),
            in_specs=[new, newt, newt] + [cache(r) for r in range(group)] * 2
            + [pl.BlockSpec((LANES, LANES), lambda b, p, pt: (0, 0)),
               pl.BlockSpec((2 * LANES, 2 * LANES), lambda b, p, pt: (0, 0))],
            out_specs=new,
            scratch_shapes=[pltpu.VMEM((LANES, D), BF16), pltpu.VMEM((LANES, D), F32),
                            pltpu.VMEM((LANES, LANES), F32)]),
        out_shape=jax.ShapeDtypeStruct((n, D), F32),
        compiler_params=_params(("arbitrary", "arbitrary")),
        name="sb_sample",
    )(page_table, q, knt, vnt, *([cache_kt] * group), *([cache_vt] * group), bias_rows, _suffix_ones())


def _lanes(col, tm):
    return jnp.concatenate([col] * (tm // LANES), axis=1)


def _rw_proj_body(h_ref, prev_ref, mu_ref, wr_ref, wk_ref, wv_ref, w1_ref, w2_ref, a1_ref,
                  a2_ref, g1_ref, g2_ref, w0_ref, a0_ref, kk_ref, ka_ref,
                  r_out, d_out, lw_out, k_out, kkn_out, b_out, v_out, g_out, *, tm):
    h = h_ref[...]
    xx = prev_ref[...] - h
    xr, xw, xk, xv, xa, xg = ((h + xx * mu_ref[n:n + 1, :]).astype(BF16) for n in range(6))
    r_out[...] = _dot_nt(wr_ref[...], xr)
    k = _dot_nt(wk_ref[...], xk)
    v_out[...] = _dot_nt(wv_ref[...], xv)
    w_lin = _lanes(w0_ref[...], tm) + _dot_nt(w2_ref[...], jnp.tanh(_dot(xw, w1_ref[...])).astype(BF16))
    w = -_softplus(-w_lin) - 0.5
    log_decay = -jnp.exp(w)
    lw_out[...] = log_decay
    d_out[...] = jnp.exp(log_decay)
    a = jax.nn.sigmoid(_lanes(a0_ref[...], tm) + _dot_nt(a2_ref[...], _dot(xa, a1_ref[...]).astype(BF16)))
    g_out[...] = _dot_nt(g2_ref[...], jax.nn.sigmoid(_dot(xg, g1_ref[...])).astype(BF16))
    kk = (k * _lanes(kk_ref[...], tm)).reshape(HEADS, HD, tm)
    norm = jnp.sqrt(jnp.sum(kk * kk, axis=1, keepdims=True))
    kk = (kk / jnp.maximum(norm, 1e-12)).reshape(D, tm)
    kkn_out[...] = kk
    b_out[...] = kk * a
    k_out[...] = k * (1.0 + (a - 1.0) * _lanes(ka_ref[...], tm))


def _rw_proj(h, prev, mats, cols, tm):
    n = h.shape[0]
    row = pl.BlockSpec((tm, D), lambda i: (i, 0))
    out = pl.BlockSpec((D, tm), lambda i: (0, i))
    return pl.pallas_call(
        functools.partial(_rw_proj_body, tm=tm),
        grid=(n // tm,),
        in_specs=[row, row] + [_full_spec(m) for m in mats] + [_full_spec(c) for c in cols],
        out_specs=[out] * 8,
        out_shape=[jax.ShapeDtypeStruct((D, n), F32)] * 8,
        compiler_params=_params(("arbitrary",)),
        name="rw_proj",
    )(h, prev, *mats, *cols)


def _chunk_body(r_ref, lw_ref, k_ref, kk_ref, b_ref, v_ref, y_ref, sT_ref, sbd_ref, ybuf_ref,
                *, n_chunks):
    c = pl.program_id(2)

    @pl.when(c == 0)
    def _():
        sbd_ref[...] = jnp.zeros_like(sbd_ref)

    m_a = lax.broadcasted_iota(jnp.int32, (CHUNK, LANES), 1) < HD
    rr = lax.broadcasted_iota(jnp.int32, (LANES, LANES), 0)
    cc = lax.broadcasted_iota(jnp.int32, (LANES, LANES), 1)
    same = (rr // CHUNK) == (cc // CHUNK)
    strict = same & (cc < rr)
    incl = same & (cc <= rr)
    ident = rr == cc
    tr = lax.broadcasted_iota(jnp.int32, (CHUNK, CHUNK), 0)
    tc = lax.broadcasted_iota(jnp.int32, (CHUNK, CHUNK), 1)
    tri = jnp.where(tc <= tr, 1.0, 0.0).astype(BF16)

    def stack(x):
        return jnp.concatenate([jnp.where(m_a, x, 0.0), jnp.where(m_a, 0.0, x)], axis=0)

    tiles = [ref[...].T for ref in (r_ref, lw_ref, k_ref, kk_ref, b_ref, v_ref)]

    ch = range(n_chunks)
    sls = [slice(n * CHUNK, (n + 1) * CHUNK) for n in ch]
    r, lw, k, kk, b, v = ([x[sl] for sl in sls] for x in tiles)
    lw_parts = [_split(x) for x in lw]
    cum = [_dot(tri, hi) + _dot(tri, lo) for hi, lo in lw_parts]
    cum_end = [x[CHUNK - 1:CHUNK, :] for x in cum]
    e_neg = [jnp.exp(-x) for x in cum]
    e_end = [jnp.exp(ce - x) for ce, x in zip(cum_end, cum)]
    kap = [_split(stack(kk[n] * jnp.exp(cum[n] - lw[n]))) for n in ch]
    rti = [_split(stack(r[n] * jnp.exp(cum[n]))) for n in ch]
    v_s = [_split(stack(x)) for x in v]
    bet = [_split(stack(b[n] * e_neg[n])) for n in ch]
    kti = [_split(stack(k[n] * e_neg[n])) for n in ch]
    gram = [_dot3_nt(_rows(kap[n], rti[n]), _rows(bet[n], kti[n])) for n in ch]
    a_k = [_split(jnp.where(strict, x[0:LANES, LANES:], 0.0)) for x in gram]
    p_k = [_split(jnp.where(incl, x[LANES:, LANES:], 0.0)) for x in gram]
    p_b = [_split(jnp.where(incl, -x[LANES:, 0:LANES], 0.0)) for x in gram]

    a_b = [jnp.where(strict, x[0:LANES, 0:LANES], 0.0) for x in gram]
    t_inv = [jnp.where(ident, 1.0, jnp.where((rr // 2) == (cc // 2), -x, 0.0)) for x in a_b]
    size = 2
    while size < CHUNK:
        join = ((rr // (2 * size)) == (cc // (2 * size))) & ((rr // size) != (cc // size))
        t_parts = [_split(x) for x in t_inv]
        tb = [_dot3(t_parts[n], _split(jnp.where(join, a_b[n], 0.0))) for n in ch]
        t_inv = [t_inv[n] - _dot3(_split(tb[n]), t_parts[n]) for n in ch]
        size *= 2
    t_low = [_split(jnp.where(ident, 0.0, x)) for x in t_inv]

    kap_f = [stack(kk[n] * jnp.exp(cum[n] - lw[n])) for n in ch]
    w = [_split(kap_f[n] + _dot3(t_low[n], kap[n])) for n in ch]
    av = [_dot3(a_k[n], v_s[n]) for n in ch]
    u = [av[n] + _dot3(t_low[n], _split(av[n])) for n in ch]
    u_s = [_split(x) for x in u]
    khat_t = [_split(stack(k[n] * e_end[n]).T) for n in ch]
    bhat_t = [_split(stack(b[n] * e_end[n]).T) for n in ch]
    m = [_split(_dot3(bhat_t[n], w[n])) for n in ch]
    const = [_dot3(khat_t[n], v_s[n]) - _dot3(bhat_t[n], u_s[n]) for n in ch]
    g_col = [jnp.broadcast_to(jnp.exp(x), (LANES, LANES)).T for x in cum_end]

    sbd = sbd_ref[...]
    for n in ch:
        s_s = _split(sbd)
        d_s = _split(_dot3(w[n], s_s) + u[n])
        y_s = _dot3(rti[n], s_s) + _dot3(p_k[n], v_s[n]) + _dot3(p_b[n], d_s)
        ybuf_ref[sls[n], :] = y_s[0:CHUNK] + y_s[CHUNK:]
        sbd = g_col[n] * sbd - _dot3(m[n], s_s) + const[n]
    sbd_ref[...] = sbd

    y_ref[...] = ybuf_ref[...].T

    @pl.when(c == pl.num_programs(2) - 1)
    def _():
        s_t = sbd.T
        top = lax.broadcasted_iota(jnp.int32, (LANES, HD), 0) < HD
        sT_ref[...] = jnp.where(top, s_t[:, 0:HD], s_t[:, HD:])


def _rw_chunk_prompt(r, lw, k, kk, b, v, batch, seq, n_chunks=8):
    width = n_chunks * CHUNK
    assert CHUNK == HD and seq % width == 0 and width % LANES == 0
    nc = seq // width
    tile = pl.BlockSpec((LANES, width), lambda p, bb, c: (p, bb * nc + c))
    st = pl.BlockSpec((None, None, LANES, HD), lambda p, bb, c: (bb, p, 0, 0))
    return pl.pallas_call(
        functools.partial(_chunk_body, n_chunks=n_chunks),
        grid=(PAIRS, batch, nc),
        in_specs=[tile] * 6,
        out_specs=[tile, st],
        out_shape=[jax.ShapeDtypeStruct(r.shape, F32),
                   jax.ShapeDtypeStruct((batch, PAIRS, LANES, HD), F32)],
        scratch_shapes=[pltpu.VMEM((LANES, LANES), F32), pltpu.VMEM((width, LANES), F32)],
        compiler_params=_params(("arbitrary", "arbitrary", "arbitrary")),
        name="rw_chunk_prompt",
    )(r, lw, k, kk, b, v)


def _scan_body(r_ref, d_ref, k_ref, kk_ref, b_ref, v_ref, s0_ref, y_ref, sT_ref,
               vbuf_ref, ybuf_ref, *, n_groups):
    first = lax.broadcasted_iota(jnp.int32, (HD, LANES), 1) < HD
    vbuf_ref[...] = v_ref[...].T

    def group(g, carry):
        base = pl.multiple_of((g // (LANES // GROUP)) * LANES, LANES)
        gl = g % (LANES // GROUP)
        shift = jnp.where(gl == 0, 0, LANES - gl * GROUP)
        tiles = [pltpu.roll(ref[:, pl.ds(base, LANES)], shift, 1)
                 for ref in (d_ref, kk_ref, b_ref, k_ref, r_ref)]
        s_t = s0_ref[g].T
        for s in range(GROUP):
            dc, kkc, bc, kc, rc = (jnp.where(first, x[0:HD, s:s + 1], x[HD:2 * HD, s:s + 1])
                                   for x in tiles)
            t = g * GROUP + s
            sa = jnp.sum(s_t * kkc, axis=0, keepdims=True)
            s_t = s_t * dc - bc * sa + kc * vbuf_ref[pl.ds(t, 1), :]
            ybuf_ref[pl.ds(t, 1), :] = jnp.sum(s_t * rc, axis=0, keepdims=True)
        sT_ref[g] = s_t.T
        return carry

    lax.fori_loop(0, n_groups, group, 0)
    y_ref[...] = ybuf_ref[...].T


def _rw_scan_sample(r, d, k, kk, b, v, s0, batch, seq):
    assert seq == GROUP
    n = batch * seq
    tile = pl.BlockSpec((LANES, n), lambda p: (p, 0))
    st = pl.BlockSpec((batch, None, LANES, HD), lambda p: (0, p, 0, 0))
    return pl.pallas_call(
        functools.partial(_scan_body, n_groups=batch),
        grid=(PAIRS,),
        in_specs=[tile] * 6 + [st],
        out_specs=[tile, st],
        out_shape=[jax.ShapeDtypeStruct(r.shape, F32),
                   jax.ShapeDtypeStruct((batch, PAIRS, LANES, HD), F32)],
        scratch_shapes=[pltpu.VMEM((n, LANES), F32), pltpu.VMEM((n, LANES), F32)],
        compiler_params=_params(("arbitrary",)),
        name="rw_scan_sample",
    )(r, d, k, kk, b, v, s0)


def _rw_out_body(y_ref, r_ref, k_ref, v_ref, g_ref, x_ref, gate_ref, wo_ref, lng_ref, lnb_ref,
                 rk_ref, o_ref, *, tm):
    y = y_ref[...].reshape(HEADS, HD, tm)
    mean = jnp.mean(y, axis=1, keepdims=True)
    var = jnp.mean(jnp.square(y - mean), axis=1, keepdims=True)
    y = ((y - mean) * lax.rsqrt(var + GN_EPS)).reshape(D, tm) * _lanes(lng_ref[...], tm) \
        + _lanes(lnb_ref[...], tm)
    rk = (r_ref[...] * k_ref[...] * _lanes(rk_ref[...], tm)).reshape(HEADS, HD, tm)
    bonus = jnp.sum(rk, axis=1, keepdims=True) * v_ref[...].reshape(HEADS, HD, tm)
    yg = (y + bonus.reshape(D, tm)) * g_ref[...]
    o = _dot(yg.T.astype(BF16), wo_ref[...])
    o_ref[...] = x_ref[...] + gate_ref[0] * o


def _rw_out(y, r, k, v, g, x, gate, wo, cols, tm, tiles_per_seq):
    n = x.shape[0]
    row = pl.BlockSpec((tm, D), lambda i: (i, 0))
    cm = pl.BlockSpec((D, tm), lambda i: (0, i))
    return pl.pallas_call(
        functools.partial(_rw_out_body, tm=tm),
        grid=(n // tm,),
        in_specs=[cm] * 5 + [row, _mod_spec(gate, tiles_per_seq), _full_spec(wo)]
        + [_full_spec(c) for c in cols],
        out_specs=row,
        out_shape=jax.ShapeDtypeStruct((n, D), F32),
        compiler_params=_params(("arbitrary",)),
        name="rw_out",
    )(y, r, k, v, g, x, gate, wo, *cols)


def _col(p):
    return jnp.broadcast_to(p.reshape(D, 1), (D, LANES))


def kernel(x_prompt, x_sample, cache_k, cache_v, state_wkv, state_shift, page_table, c_prompt, c_sample, norm_mix, ada_w_mix, ada_b_mix, norm_ffn, ada_w_ffn, ada_b_ffn, w_up, w_down, sb_w_qkv, sb_w_o, sb_bias, rw_mu, rw_w_r, rw_w_k, rw_w_v, rw_w_o, rw_w0, rw_w1, rw_w2, rw_a0, rw_a1, rw_a2, rw_g1, rw_g2, rw_k_k, rw_k_a, rw_r_k, rw_ln_g, rw_ln_b, final_norm):
    batch, seq, _ = x_prompt.shape
    dbatch, dseq, _ = x_sample.shape
    depth = norm_mix.shape[0]
    n_p, n_s = batch * seq, dbatch * dseq
    tm_p, tm_s = 512, n_s
    tps_p, tps_s = seq // tm_p, 1

    mod_mix = _ada_mod(jnp.concatenate([c_prompt, c_sample], axis=0), ada_w_mix, ada_b_mix)
    mod_ffn = _ada_mod(jnp.concatenate([c_prompt, c_sample], axis=0), ada_w_ffn, ada_b_ffn)

    def split(mod):
        parts = [mod[:, n * D:(n + 1) * D] for n in range(3)]
        return ([m[:batch].reshape(batch, 1, D) for m in parts],
                [jnp.repeat(m[batch:], dseq, axis=0).reshape(1, n_s, D) for m in parts])

    xp = x_prompt.reshape(n_p, D)
    xs = x_sample.reshape(n_s, D)
    fin = final_norm.reshape(1, D)
    outs = {}

    for i in range(depth):
        j = i // 2
        g_mix = norm_mix[i].reshape(1, D)
        (sh_p, sc_p, ga_p), (sh_s, sc_s, ga_s) = split(mod_mix[i])
        if i % 2 == 0:
            w_qkv = sb_w_qkv[j].astype(BF16)
            w_o = sb_w_o[j].astype(BF16)
            qp, ktp, vtp, ktbp, _, vbp = _qkv(xp, g_mix, sc_p, sh_p, w_qkv, tm_p, tps_p, BF16)
            qs, kts, vts, _, ks, vs = _qkv(xs, g_mix, sc_s, sh_s, w_qkv, tm_s, tps_s, F32)
            op = _sb_prompt(qp, ktbp, vbp, sb_bias[j], batch, seq, 256)
            n_phys, page = cache_k.shape[1], cache_k.shape[2]
            cache_kt = jnp.transpose(cache_k[j], (0, 2, 3, 1)).reshape(n_phys, D, page)
            cache_vt = jnp.transpose(cache_v[j], (0, 2, 3, 1)).reshape(n_phys, D, page)
            os_ = _sb_sample(qs, kts[0], vts[0], cache_kt, cache_vt, page_table, sb_bias[j], dseq)
            xp = _proj_res(op, xp, ga_p, w_o, tm_p, tps_p)
            xs = _proj_res(os_, xs, ga_s, w_o, tm_s, tps_s)
            outs.setdefault("kp", []).append(
                jnp.transpose(ktp.reshape(batch, HEADS, HD, seq), (0, 3, 1, 2)))
            outs.setdefault("vp", []).append(
                jnp.transpose(vtp.reshape(batch, HEADS, HD, seq), (0, 3, 1, 2)))
            outs.setdefault("ks", []).append(ks.reshape(dbatch, dseq, HEADS, HD))
            outs.setdefault("vs", []).append(vs.reshape(dbatch, dseq, HEADS, HD))
        else:
            pad = GATE_LORA_PAD - rw_g1.shape[-1]
            mats = (jnp.pad(rw_mu[j], ((0, 2), (0, 0))),
                    rw_w_r[j].T.astype(BF16), rw_w_k[j].T.astype(BF16), rw_w_v[j].T.astype(BF16),
                    rw_w1[j].astype(BF16), rw_w2[j].T.astype(BF16),
                    rw_a1[j].astype(BF16), rw_a2[j].T.astype(BF16),
                    jnp.pad(rw_g1[j], ((0, 0), (0, pad))).astype(BF16),
                    jnp.pad(rw_g2[j], ((0, pad), (0, 0))).T.astype(BF16))
            cols = (_col(rw_w0[j]), _col(rw_a0[j]), _col(rw_k_k[j]), _col(rw_k_a[j]))
            ocols = (_col(rw_ln_g[j]), _col(rw_ln_b[j]), _col(rw_r_k[j]))
            w_o = rw_w_o[j].astype(BF16)

            hp = _norm(xp, g_mix, sc_p, sh_p, tm_p, tps_p).reshape(batch, seq, D)
            hs = _norm(xs, g_mix, sc_s, sh_s, tm_s, tps_s).reshape(dbatch, dseq, D)
            prev_p = jnp.concatenate([jnp.zeros((batch, 1, D), F32), hp[:, :-1]], axis=1)
            prev_s = jnp.concatenate([state_shift[j][:, None, :], hs[:, :-1]], axis=1)

            rp, _, lwp, kp_, kkp, bp, vp_, gp = _rw_proj(hp.reshape(n_p, D), prev_p.reshape(n_p, D),
                                                         mats, cols, 256)
            rs, ds, _, ks_, kks, bs, vs_, gs = _rw_proj(hs.reshape(n_s, D), prev_s.reshape(n_s, D),
                                                        mats, cols, n_s)
            yp, wkv_p = _rw_chunk_prompt(rp, lwp, kp_, kkp, bp, vp_, batch, seq)
            s0 = state_wkv[j].reshape(dbatch, PAIRS, LANES, HD)
            ys, wkv_s = _rw_scan_sample(rs, ds, ks_, kks, bs, vs_, s0, dbatch, dseq)
            xp = _rw_out(yp, rp, kp_, vp_, gp, xp, ga_p, w_o, ocols, 256, seq // 256)
            xs = _rw_out(ys, rs, ks_, vs_, gs, xs, ga_s, w_o, ocols, n_s, 1)
            outs.setdefault("wp", []).append(wkv_p.reshape(batch, HEADS, HD, HD))
            outs.setdefault("sp", []).append(hp[:, -1])
            outs.setdefault("ws", []).append(wkv_s.reshape(dbatch, HEADS, HD, HD))
            outs.setdefault("ss", []).append(hs[:, -1])

        g_ffn = norm_ffn[i].reshape(1, D)
        (sh_p, sc_p, ga_p), (sh_s, sc_s, ga_s) = split(mod_ffn[i])
        wu = w_up[i].astype(BF16)
        wd = w_down[i].astype(BF16)
        last = i == depth - 1
        xp = _mlp(xp, g_ffn, sc_p, sh_p, ga_p, wu, wd, fin, tm_p, tps_p, last)
        xs = _mlp(xs, g_ffn, sc_s, sh_s, ga_s, wu, wd, fin, tm_s, tps_s, last)

    return (xp.reshape(batch, seq, D), xs.reshape(dbatch, dseq, D),
            jnp.stack(outs["kp"]), jnp.stack(outs["vp"]), jnp.stack(outs["ks"]),
            jnp.stack(outs["vs"]), jnp.stack(outs["wp"]), jnp.stack(outs["sp"]),
            jnp.stack(outs["ws"]), jnp.stack(outs["ss"]))
```

```python
import functools

import jax
import jax.numpy as jnp
from jax import lax
from jax.experimental import pallas as pl
from jax.experimental.pallas import tpu as pltpu

F32 = jnp.float32
BF16 = jnp.bfloat16

D = 1024
HEADS = 16
HD = 64
PAIRS = HEADS // 2
LANES = 128
DFF = 4 * D
NORM_EPS = 1e-6
GN_EPS = 1e-5 * HD
LOG2E = 1.4426950408889634
Q_SCALE = HD ** -0.5 * LOG2E
GATE_LORA_PAD = 256
GROUP = 8
CHUNK = 64
VMEM_LIMIT = 56 * 1024 * 1024

NT = (((1,), (1,)), ((), ()))


def _params(sem):
    return pltpu.CompilerParams(dimension_semantics=sem, vmem_limit_bytes=VMEM_LIMIT)


def _dot(a, b):
    return jnp.dot(a, b, preferred_element_type=F32)


def _dot_nt(a, b):
    return lax.dot_general(a, b, NT, preferred_element_type=F32)


def _softplus(x):
    return jnp.maximum(x, 0.0) + jnp.log(1.0 + jnp.exp(-jnp.abs(x)))


def _adanorm(x, g, scale, shift):
    y = x * lax.rsqrt(jnp.mean(x * x, axis=-1, keepdims=True) + NORM_EPS)
    return (y * g) * (1.0 + scale) + shift


def _mod_spec(mod, tiles_per_seq):
    rows = mod.shape[1]
    return pl.BlockSpec((1, rows, D), lambda i: (i // tiles_per_seq, 0, 0))


def _full_spec(a):
    nd = a.ndim
    return pl.BlockSpec(a.shape, lambda i: (0,) * nd, pipeline_mode=pl.Buffered(1))


def _ada_body(c_ref, w_ref, b_ref, o_ref):
    c = c_ref[...]
    s = (c * jax.nn.sigmoid(c)).astype(BF16)
    o_ref[0] = _dot(s, w_ref[0].astype(BF16)) + b_ref[0]


def _ada_mod(c_all, w, b):
    layers, rows, tn = w.shape[0], c_all.shape[0], 768
    return pl.pallas_call(
        _ada_body,
        grid=(layers, 3 * D // tn),
        in_specs=[pl.BlockSpec((rows, D), lambda l, j: (0, 0)),
                  pl.BlockSpec((1, D, tn), lambda l, j: (l, 0, j)),
                  pl.BlockSpec((1, 1, tn), lambda l, j: (l, 0, j))],
        out_specs=pl.BlockSpec((1, rows, tn), lambda l, j: (l, 0, j)),
        out_shape=jax.ShapeDtypeStruct((layers, rows, 3 * D), F32),
        compiler_params=_params(("arbitrary", "arbitrary")),
        name="ada_mod",
    )(c_all, w, b.reshape(layers, 1, 3 * D))


def _qkv_body(x_ref, g_ref, sc_ref, sh_ref, wq_ref, wkt_ref, wvt_ref, wk_ref, wv_ref,
              q_ref, kt_ref, vt_ref, ktb_ref, k_ref, v_ref):
    h = _adanorm(x_ref[...], g_ref[...], sc_ref[0], sh_ref[0]).astype(BF16)
    q_ref[...] = _dot(h, wq_ref[...]) * Q_SCALE
    kt = _dot_nt(wkt_ref[...], h)
    kt_ref[...] = kt
    ktb_ref[...] = kt.astype(BF16)
    vt_ref[...] = _dot_nt(wvt_ref[...], h)
    k_ref[...] = _dot(h, wk_ref[...]).astype(k_ref.dtype)
    v_ref[...] = _dot(h, wv_ref[...]).astype(v_ref.dtype)


def _qkv(x, g, scale, shift, w, tm, tiles_per_seq, row_dtype):
    n = x.shape[0]
    seq = tm * tiles_per_seq
    wq, wk, wv = w[:, 0:D], w[:, D:2 * D], w[:, 2 * D:3 * D]
    row = pl.BlockSpec((tm, D), lambda i: (i, 0))
    cm = pl.BlockSpec((None, D, tm), lambda i: (i // tiles_per_seq, 0, i % tiles_per_seq))
    return pl.pallas_call(
        _qkv_body,
        grid=(n // tm,),
        in_specs=[row, _full_spec(g), _mod_spec(scale, tiles_per_seq),
                  _mod_spec(shift, tiles_per_seq)] + [_full_spec(wq)] * 5,
        out_specs=[row, cm, cm, cm, row, row],
        out_shape=[jax.ShapeDtypeStruct((n, D), F32),
                   jax.ShapeDtypeStruct((n // seq, D, seq), F32),
                   jax.ShapeDtypeStruct((n // seq, D, seq), F32),
                   jax.ShapeDtypeStruct((n // seq, D, seq), BF16),
                   jax.ShapeDtypeStruct((n, D), row_dtype),
                   jax.ShapeDtypeStruct((n, D), row_dtype)],
        compiler_params=_params(("arbitrary",)),
        name="qkv",
    )(x, g, scale, shift, wq, wk.T, wv.T, wk, wv)


def _proj_res_body(o_ref, x_ref, gate_ref, w_ref, y_ref):
    y_ref[...] = x_ref[...] + gate_ref[0] * _dot(o_ref[...].astype(BF16), w_ref[...])


def _proj_res(o, x, gate, w, tm, tiles_per_seq):
    n = x.shape[0]
    row = pl.BlockSpec((tm, D), lambda i: (i, 0))
    return pl.pallas_call(
        _proj_res_body,
        grid=(n // tm,),
        in_specs=[row, row, _mod_spec(gate, tiles_per_seq), _full_spec(w)],
        out_specs=row,
        out_shape=jax.ShapeDtypeStruct((n, D), F32),
        compiler_params=_params(("arbitrary",)),
        name="proj_res",
    )(o, x, gate, w)


def _mlp_body(x_ref, g_ref, sc_ref, sh_ref, gate_ref, wu_ref, wd_ref, fin_ref, y_ref,
              *, final_norm):
    x = x_ref[...]
    h = _adanorm(x, g_ref[...], sc_ref[0], sh_ref[0]).astype(BF16)
    acc = jnp.zeros_like(x)
    for c in range(DFF // D):
        u = jnp.maximum(_dot(h, wu_ref[:, c * D:(c + 1) * D]), 0.0)
        acc = acc + _dot((u * u).astype(BF16), wd_ref[c * D:(c + 1) * D, :])
    y = x + gate_ref[0] * acc
    if final_norm:
        y = (y * lax.rsqrt(jnp.mean(y * y, axis=-1, keepdims=True) + NORM_EPS)) * fin_ref[...]
    y_ref[...] = y


def _mlp(x, g, scale, shift, gate, wu, wd, fin, tm, tiles_per_seq, final_norm):
    n = x.shape[0]
    row = pl.BlockSpec((tm, D), lambda i: (i, 0))
    return pl.pallas_call(
        functools.partial(_mlp_body, final_norm=final_norm),
        grid=(n // tm,),
        in_specs=[row, _full_spec(g), _mod_spec(scale, tiles_per_seq),
                  _mod_spec(shift, tiles_per_seq), _mod_spec(gate, tiles_per_seq),
                  _full_spec(wu), _full_spec(wd), _full_spec(fin)],
        out_specs=row,
        out_shape=jax.ShapeDtypeStruct((n, D), F32),
        compiler_params=_params(("arbitrary",)),
        name="mlp",
    )(x, g, scale, shift, gate, wu, wd, fin)


def _norm_body(x_ref, g_ref, sc_ref, sh_ref, h_ref):
    h_ref[...] = _adanorm(x_ref[...], g_ref[...], sc_ref[0], sh_ref[0])


def _norm(x, g, scale, shift, tm, tiles_per_seq):
    n = x.shape[0]
    row = pl.BlockSpec((tm, D), lambda i: (i, 0))
    return pl.pallas_call(
        _norm_body,
        grid=(n // tm,),
        in_specs=[row, _full_spec(g), _mod_spec(scale, tiles_per_seq),
                  _mod_spec(shift, tiles_per_seq)],
        out_specs=row,
        out_shape=jax.ShapeDtypeStruct((n, D), F32),
        compiler_params=_params(("arbitrary",)),
        name="ada_norm",
    )(x, g, scale, shift)


def _suffix_ones():
    j = lax.broadcasted_iota(jnp.int32, (2 * LANES, 2 * LANES), 0)
    s = lax.broadcasted_iota(jnp.int32, (2 * LANES, 2 * LANES), 1)
    return jnp.where(j >= s, 1.0, 0.0).astype(BF16)


def _split(x):
    hi = x.astype(BF16)
    return hi, (x - hi.astype(F32)).astype(BF16)


def _rows(x, y):
    return tuple(jnp.concatenate([a, b], axis=0) for a, b in zip(x, y))


def _dot3(x, y):
    (xh, xl), (yh, yl) = x, y
    return _dot(jnp.concatenate([xh, xh, xl], axis=1), jnp.concatenate([yh, yl, yh], axis=0))


def _dot3_nt(x, y):
    (xh, xl), (yh, yl) = x, y
    return _dot_nt(jnp.concatenate([xh, xh, xl], axis=1), jnp.concatenate([yh, yl, yh], axis=1))


def _sb_weights(z, mask, uo, carry):
    n = z.shape[1] // LANES
    sp = jnp.maximum(z, 0.0) + jnp.log(1.0 + jnp.exp2(-jnp.abs(z))) * LOG2E
    if mask is not None:
        sp = jnp.where(mask, sp, 0.0)
    sp_b = sp.astype(BF16)
    if n == 1:
        cs = _dot(sp_b, uo[LANES:, :])
        a = jnp.exp2(z - cs[:, LANES:] - carry)
        carry = carry + cs[:, :LANES]
    else:
        parts = [None] * (n // 2)
        for s in reversed(range(n // 2)):
            sl = slice(2 * s * LANES, 2 * (s + 1) * LANES)
            cs = _dot(sp_b[:, sl], uo)
            parts[s] = jnp.exp2(z[:, sl] - cs - jnp.concatenate([carry, carry], axis=1))
            carry = carry + jnp.broadcast_to(cs[:, 0:1], carry.shape)
        a = parts[0] if n == 2 else jnp.concatenate(parts, axis=1)
    if mask is not None:
        a = jnp.where(mask, a, 0.0)
    return a, carry


def _sbp_body(bias_ref, q_ref, k_ref, v_ref, uo_ref, o_ref, acc_ref, car_ref, *, tq):
    p = pl.program_id(1)
    i = pl.program_id(2)
    wide = 2 * tq
    lane = lax.broadcasted_iota(jnp.int32, (tq, LANES), 1)
    q = q_ref[...]
    qh = (jnp.where(lane < HD, q, 0.0).astype(BF16), jnp.where(lane >= HD, q, 0.0).astype(BF16))
    acc_ref[...] = jnp.zeros_like(acc_ref)
    car_ref[...] = jnp.zeros_like(car_ref)
    uo = uo_ref[...]

    def block(off, width, masked):
        kt = k_ref[:, pl.ds(off, width)]
        v = v_ref[pl.ds(off, width), :]
        mask = None
        if masked:
            mask = (lax.broadcasted_iota(jnp.int32, (tq, width), 1)
                    < lax.broadcasted_iota(jnp.int32, (tq, width), 0))
        zs = [_dot(qh[hh], kt) + bias_ref[2 * p + hh] * LOG2E for hh in range(2)]
        ws = [_sb_weights(zs[hh], mask, uo, car_ref[hh]) for hh in range(2)]
        for hh in range(2):
            acc_ref[hh] += _dot(ws[hh][0].astype(BF16), v)
            car_ref[hh] = ws[hh][1]

    block(pl.multiple_of(i * tq, tq), tq, True)

    @pl.when(i % 2 == 1)
    def _():
        block(pl.multiple_of((i - 1) * tq, tq), tq, False)

    def body(n, c):
        block(pl.multiple_of((i // 2 - 1 - n) * wide, wide), wide, False)
        return c

    lax.fori_loop(0, i // 2, body, 0)
    o_ref[...] = jnp.where(lane < HD, acc_ref[0], acc_ref[1]).astype(o_ref.dtype)


def _sb_prompt(q, ktb, vb, bias, batch, seq, tq):
    n = q.shape[0]
    nq = seq // tq
    return pl.pallas_call(
        functools.partial(_sbp_body, tq=tq),
        grid=(batch, PAIRS, nq),
        in_specs=[pl.BlockSpec(memory_space=pltpu.SMEM),
                  pl.BlockSpec((tq, LANES), lambda b, p, i: (b * nq + i, p)),
                  pl.BlockSpec((None, LANES, seq), lambda b, p, i: (b, p, 0)),
                  pl.BlockSpec((seq, LANES), lambda b, p, i: (b, p)),
                  pl.BlockSpec((2 * LANES, 2 * LANES), lambda b, p, i: (0, 0))],
        out_specs=pl.BlockSpec((tq, LANES), lambda b, p, i: (b * nq + i, p)),
        out_shape=jax.ShapeDtypeStruct((n, D), BF16),
        scratch_shapes=[pltpu.VMEM((2, tq, LANES), F32), pltpu.VMEM((2, tq, LANES), F32)],
        compiler_params=_params(("arbitrary", "arbitrary", "arbitrary")),
        name="sb_prompt",
    )(bias, q, ktb, vb, _suffix_ones())


def _sbs_body(pt_ref, q_ref, knt_ref, vnt_ref, *rest, n_new, n_pages, group):
    del pt_ref
    kc_refs, vc_refs = rest[:group], rest[group:2 * group]
    bias_ref, uo_ref, o_ref, qb_ref, acc_ref, car_ref = rest[2 * group:]
    b = pl.program_id(0)
    p = pl.program_id(1)
    row = lax.broadcasted_iota(jnp.int32, (LANES, D), 0)
    col = lax.broadcasted_iota(jnp.int32, (LANES, D), 1)
    own = (row // n_new) == (col // HD)
    uo = uo_ref[...]

    def sweep(kts, vts, mask):
        z = jnp.concatenate([_dot(qb_ref[...], kt.astype(BF16)) + bias_ref[...] for kt in kts], axis=1)
        a, car = _sb_weights(z, mask, uo, car_ref[...])
        car_ref[...] = car
        a = a.astype(BF16)
        o = _dot_nt(a[:, 0:LANES], vts[0].astype(BF16))
        for r in range(1, len(vts)):
            o = o + _dot_nt(a[:, r * LANES:(r + 1) * LANES], vts[r].astype(BF16))
        acc_ref[...] += o

    @pl.when(p == 0)
    def _():
        q = q_ref[...]
        qt = jnp.broadcast_to(q[None], (HEADS, n_new, D)).reshape(LANES, D)
        qb_ref[...] = jnp.where(own, qt, 0.0).astype(BF16)
        acc_ref[...] = jnp.zeros_like(acc_ref)
        car_ref[...] = jnp.zeros_like(car_ref)
        off = (b % (LANES // n_new)) * n_new
        r = lax.broadcasted_iota(jnp.int32, (LANES, LANES), 0)
        s = lax.broadcasted_iota(jnp.int32, (LANES, LANES), 1) - off
        sweep([knt_ref[...]], [vnt_ref[...]], (s >= 0) & (s < (r % n_new)))

    sweep([r[...] for r in kc_refs], [r[...] for r in vc_refs], None)

    @pl.when(p == n_pages // group - 1)
    def _():
        o = jnp.where(own, acc_ref[...], 0.0).reshape(HEADS, n_new, D)
        o_ref[...] = jnp.sum(o, axis=0)


def _sb_sample(q, knt, vnt, cache_kt, cache_vt, page_table, bias, n_new, group=8):
    n = q.shape[0]
    batch, n_pages = page_table.shape
    page = cache_kt.shape[2]
    assert page == LANES and HEADS * n_new == LANES and n_pages % group == 0 and n % LANES == 0
    bias_rows = jnp.broadcast_to(jnp.repeat(bias * LOG2E, n_new)[:, None], (LANES, LANES))
    per_tile = LANES // n_new
    new = pl.BlockSpec((n_new, D), lambda b, p, pt: (b, 0))
    newt = pl.BlockSpec((D, LANES), lambda b, p, pt: (0, b // per_tile))

    def cache(r):
        return pl.BlockSpec((None, D, page),
                            lambda b, p, pt: (pt[b, n_pages - group * (p + 1) + r], 0, 0))

    return pl.pallas_call(
        functools.partial(_sbs_body, n_new=n_new, n_pages=n_pages, group=group),
        grid_spec=pltpu.PrefetchScalarGridSpec(
            num_scalar_prefetch=1,
            grid=(batch, n_pages // group),
            in_specs=[new, newt, newt] + [cache(r) for r in range(group)] * 2
            + [pl.BlockSpec((LANES, LANES), lambda b, p, pt: (0, 0)),
               pl.BlockSpec((2 * LANES, 2 * LANES), lambda b, p, pt: (0, 0))],
            out_specs=new,
            scratch_shapes=[pltpu.VMEM((LANES, D), BF16), pltpu.VMEM((LANES, D), F32),
                            pltpu.VMEM((LANES, LANES), F32)]),
        out_shape=jax.ShapeDtypeStruct((n, D), F32),
        compiler_params=_params(("arbitrary", "arbitrary")),
        name="sb_sample",
    )(page_table, q, knt, vnt, *([cache_kt] * group), *([cache_vt] * group), bias_rows, _suffix_ones())


def _lanes(col, tm):
    return jnp.concatenate([col] * (tm // LANES), axis=1)


def _rw_proj_body(h_ref, prev_ref, mu_ref, wr_ref, wk_ref, wv_ref, w1_ref, w2_ref, a1_ref,
                  a2_ref, g1_ref, g2_ref, w0_ref, a0_ref, kk_ref, ka_ref,
                  r_out, d_out, lw_out, k_out, kkn_out, b_out, v_out, g_out, *, tm):
    h = h_ref[...]
    xx = prev_ref[...] - h
    xr, xw, xk, xv, xa, xg = ((h + xx * mu_ref[n:n + 1, :]).astype(BF16) for n in range(6))
    r_out[...] = _dot_nt(wr_ref[...], xr)
    k = _dot_nt(wk_ref[...], xk)
    v_out[...] = _dot_nt(wv_ref[...], xv)
    w_lin = _lanes(w0_ref[...], tm) + _dot_nt(w2_ref[...], jnp.tanh(_dot(xw, w1_ref[...])).astype(BF16))
    w = -_softplus(-w_lin) - 0.5
    log_decay = -jnp.exp(w)
    lw_out[...] = log_decay
    d_out[...] = jnp.exp(log_decay)
    a = jax.nn.sigmoid(_lanes(a0_ref[...], tm) + _dot_nt(a2_ref[...], _dot(xa, a1_ref[...]).astype(BF16)))
    g_out[...] = _dot_nt(g2_ref[...], jax.nn.sigmoid(_dot(xg, g1_ref[...])).astype(BF16))
    kk = (k * _lanes(kk_ref[...], tm)).reshape(HEADS, HD, tm)
    norm = jnp.sqrt(jnp.sum(kk * kk, axis=1, keepdims=True))
    kk = (kk / jnp.maximum(norm, 1e-12)).reshape(D, tm)
    kkn_out[...] = kk
    b_out[...] = kk * a
    k_out[...] = k * (1.0 + (a - 1.0) * _lanes(ka_ref[...], tm))


def _rw_proj(h, prev, mats, cols, tm):
    n = h.shape[0]
    row = pl.BlockSpec((tm, D), lambda i: (i, 0))
    out = pl.BlockSpec((D, tm), lambda i: (0, i))
    return pl.pallas_call(
        functools.partial(_rw_proj_body, tm=tm),
        grid=(n // tm,),
        in_specs=[row, row] + [_full_spec(m) for m in mats] + [_full_spec(c) for c in cols],
        out_specs=[out] * 8,
        out_shape=[jax.ShapeDtypeStruct((D, n), F32)] * 8,
        compiler_params=_params(("arbitrary",)),
        name="rw_proj",
    )(h, prev, *mats, *cols)


def _chunk_body(r_ref, lw_ref, k_ref, kk_ref, b_ref, v_ref, y_ref, sT_ref, sbd_ref, ybuf_ref,
                *, n_chunks):
    c = pl.program_id(2)

    @pl.when(c == 0)
    def _():
        sbd_ref[...] = jnp.zeros_like(sbd_ref)

    m_a = lax.broadcasted_iota(jnp.int32, (CHUNK, LANES), 1) < HD
    rr = lax.broadcasted_iota(jnp.int32, (LANES, LANES), 0)
    cc = lax.broadcasted_iota(jnp.int32, (LANES, LANES), 1)
    same = (rr // CHUNK) == (cc // CHUNK)
    strict = same & (cc < rr)
    incl = same & (cc <= rr)
    ident = rr == cc
    tr = lax.broadcasted_iota(jnp.int32, (CHUNK, CHUNK), 0)
    tc = lax.broadcasted_iota(jnp.int32, (CHUNK, CHUNK), 1)
    tri = jnp.where(tc <= tr, 1.0, 0.0).astype(BF16)

    def stack(x):
        return jnp.concatenate([jnp.where(m_a, x, 0.0), jnp.where(m_a, 0.0, x)], axis=0)

    tiles = [ref[...].T for ref in (r_ref, lw_ref, k_ref, kk_ref, b_ref, v_ref)]

    ch = range(n_chunks)
    sls = [slice(n * CHUNK, (n + 1) * CHUNK) for n in ch]
    r, lw, k, kk, b, v = ([x[sl] for sl in sls] for x in tiles)
    lw_parts = [_split(x) for x in lw]
    cum = [_dot(tri, hi) + _dot(tri, lo) for hi, lo in lw_parts]
    cum_end = [x[CHUNK - 1:CHUNK, :] for x in cum]
    e_neg = [jnp.exp(-x) for x in cum]
    e_end = [jnp.exp(ce - x) for ce, x in zip(cum_end, cum)]
    kap = [_split(stack(kk[n] * jnp.exp(cum[n] - lw[n]))) for n in ch]
    rti = [_split(stack(r[n] * jnp.exp(cum[n]))) for n in ch]
    v_s = [_split(stack(x)) for x in v]
    bet = [_split(stack(b[n] * e_neg[n])) for n in ch]
    kti = [_split(stack(k[n] * e_neg[n])) for n in ch]
    gram = [_dot3_nt(_rows(kap[n], rti[n]), _rows(bet[n], kti[n])) for n in ch]
    a_k = [_split(jnp.where(strict, x[0:LANES, LANES:], 0.0)) for x in gram]
    p_k = [_split(jnp.where(incl, x[LANES:, LANES:], 0.0)) for x in gram]
    p_b = [_split(jnp.where(incl, -x[LANES:, 0:LANES], 0.0)) for x in gram]

    a_b = [jnp.where(strict, x[0:LANES, 0:LANES], 0.0) for x in gram]
    t_inv = [jnp.where(ident, 1.0, jnp.where((rr // 2) == (cc // 2), -x, 0.0)) for x in a_b]
    size = 2
    while size < CHUNK:
        join = ((rr // (2 * size)) == (cc // (2 * size))) & ((rr // size) != (cc // size))
        t_parts = [_split(x) for x in t_inv]
        tb = [_dot3(t_parts[n], _split(jnp.where(join, a_b[n], 0.0))) for n in ch]
        t_inv = [t_inv[n] - _dot3(_split(tb[n]), t_parts[n]) for n in ch]
        size *= 2
    t_low = [_split(jnp.where(ident, 0.0, x)) for x in t_inv]

    kap_f = [stack(kk[n] * jnp.exp(cum[n] - lw[n])) for n in ch]
    w = [_split(kap_f[n] + _dot3(t_low[n], kap[n])) for n in ch]
    av = [_dot3(a_k[n], v_s[n]) for n in ch]
    u = [av[n] + _dot3(t_low[n], _split(av[n])) for n in ch]
    u_s = [_split(x) for x in u]
    khat_t = [_split(stack(k[n] * e_end[n]).T) for n in ch]
    bhat_t = [_split(stack(b[n] * e_end[n]).T) for n in ch]
    m = [_split(_dot3(bhat_t[n], w[n])) for n in ch]
    const = [_dot3(khat_t[n], v_s[n]) - _dot3(bhat_t[n], u_s[n]) for n in ch]
    g_col = [jnp.broadcast_to(jnp.exp(x), (LANES, LANES)).T for x in cum_end]

    sbd = sbd_ref[...]
    for n in ch:
        s_s = _split(sbd)
        d_s = _split(_dot3(w[n], s_s) + u[n])
        y_s = _dot3(rti[n], s_s) + _dot3(p_k[n], v_s[n]) + _dot3(p_b[n], d_s)
        ybuf_ref[sls[n], :] = y_s[0:CHUNK] + y_s[CHUNK:]
        sbd = g_col[n] * sbd - _dot3(m[n], s_s) + const[n]
    sbd_ref[...] = sbd

    y_ref[...] = ybuf_ref[...].T

    @pl.when(c == pl.num_programs(2) - 1)
    def _():
        s_t = sbd.T
        top = lax.broadcasted_iota(jnp.int32, (LANES, HD), 0) < HD
        sT_ref[...] = jnp.where(top, s_t[:, 0:HD], s_t[:, HD:])


def _rw_chunk_prompt(r, lw, k, kk, b, v, batch, seq, n_chunks=8):
    width = n_chunks * CHUNK
    assert CHUNK == HD and seq % width == 0 and width % LANES == 0
    nc = seq // width
    tile = pl.BlockSpec((LANES, width), lambda p, bb, c: (p, bb * nc + c))
    st = pl.BlockSpec((None, None, LANES, HD), lambda p, bb, c: (bb, p, 0, 0))
    return pl.pallas_call(
        functools.partial(_chunk_body, n_chunks=n_chunks),
        grid=(PAIRS, batch, nc),
        in_specs=[tile] * 6,
        out_specs=[tile, st],
        out_shape=[jax.ShapeDtypeStruct(r.shape, F32),
                   jax.ShapeDtypeStruct((batch, PAIRS, LANES, HD), F32)],
        scratch_shapes=[pltpu.VMEM((LANES, LANES), F32), pltpu.VMEM((width, LANES), F32)],
        compiler_params=_params(("arbitrary", "arbitrary", "arbitrary")),
        name="rw_chunk_prompt",
    )(r, lw, k, kk, b, v)


def _scan_body(r_ref, d_ref, k_ref, kk_ref, b_ref, v_ref, s0_ref, y_ref, sT_ref,
               vbuf_ref, ybuf_ref, *, n_groups):
    first = lax.broadcasted_iota(jnp.int32, (HD, LANES), 1) < HD
    vbuf_ref[...] = v_ref[...].T

    def group(g, carry):
        base = pl.multiple_of((g // (LANES // GROUP)) * LANES, LANES)
        gl = g % (LANES // GROUP)
        shift = jnp.where(gl == 0, 0, LANES - gl * GROUP)
        tiles = [pltpu.roll(ref[:, pl.ds(base, LANES)], shift, 1)
                 for ref in (d_ref, kk_ref, b_ref, k_ref, r_ref)]
        s_t = s0_ref[g].T
        for s in range(GROUP):
            dc, kkc, bc, kc, rc = (jnp.where(first, x[0:HD, s:s + 1], x[HD:2 * HD, s:s + 1])
                                   for x in tiles)
            t = g * GROUP + s
            sa = jnp.sum(s_t * kkc, axis=0, keepdims=True)
            s_t = s_t * dc - bc * sa + kc * vbuf_ref[pl.ds(t, 1), :]
            ybuf_ref[pl.ds(t, 1), :] = jnp.sum(s_t * rc, axis=0, keepdims=True)
        sT_ref[g] = s_t.T
        return carry

    lax.fori_loop(0, n_groups, group, 0)
    y_ref[...] = ybuf_ref[...].T


def _rw_scan_sample(r, d, k, kk, b, v, s0, batch, seq):
    assert seq == GROUP
    n = batch * seq
    tile = pl.BlockSpec((LANES, n), lambda p: (p, 0))
    st = pl.BlockSpec((batch, None, LANES, HD), lambda p: (0, p, 0, 0))
    return pl.pallas_call(
        functools.partial(_scan_body, n_groups=batch),
        grid=(PAIRS,),
        in_specs=[tile] * 6 + [st],
        out_specs=[tile, st],
        out_shape=[jax.ShapeDtypeStruct(r.shape, F32),
                   jax.ShapeDtypeStruct((batch, PAIRS, LANES, HD), F32)],
        scratch_shapes=[pltpu.VMEM((n, LANES), F32), pltpu.VMEM((n, LANES), F32)],
        compiler_params=_params(("arbitrary",)),
        name="rw_scan_sample",
    )(r, d, k, kk, b, v, s0)


def _rw_out_body(y_ref, r_ref, k_ref, v_ref, g_ref, x_ref, gate_ref, wo_ref, lng_ref, lnb_ref,
                 rk_ref, o_ref, *, tm):
    y = y_ref[...].reshape(HEADS, HD, tm)
    mean = jnp.mean(y, axis=1, keepdims=True)
    var = jnp.mean(jnp.square(y - mean), axis=1, keepdims=True)
    y = ((y - mean) * lax.rsqrt(var + GN_EPS)).reshape(D, tm) * _lanes(lng_ref[...], tm) \
        + _lanes(lnb_ref[...], tm)
    rk = (r_ref[...] * k_ref[...] * _lanes(rk_ref[...], tm)).reshape(HEADS, HD, tm)
    bonus = jnp.sum(rk, axis=1, keepdims=True) * v_ref[...].reshape(HEADS, HD, tm)
    yg = (y + bonus.reshape(D, tm)) * g_ref[...]
    o = _dot(yg.T.astype(BF16), wo_ref[...])
    o_ref[...] = x_ref[...] + gate_ref[0] * o


def _rw_out(y, r, k, v, g, x, gate, wo, cols, tm, tiles_per_seq):
    n = x.shape[0]
    row = pl.BlockSpec((tm, D), lambda i: (i, 0))
    cm = pl.BlockSpec((D, tm), lambda i: (0, i))
    return pl.pallas_call(
        functools.partial(_rw_out_body, tm=tm),
        grid=(n // tm,),
        in_specs=[cm] * 5 + [row, _mod_spec(gate, tiles_per_seq), _full_spec(wo)]
        + [_full_spec(c) for c in cols],
        out_specs=row,
        out_shape=jax.ShapeDtypeStruct((n, D), F32),
        compiler_params=_params(("arbitrary",)),
        name="rw_out",
    )(y, r, k, v, g, x, gate, wo, *cols)


def _col(p):
    return jnp.broadcast_to(p.reshape(D, 1), (D, LANES))


def kernel(x_prompt, x_sample, cache_k, cache_v, state_wkv, state_shift, page_table, c_prompt, c_sample, norm_mix, ada_w_mix, ada_b_mix, norm_ffn, ada_w_ffn, ada_b_ffn, w_up, w_down, sb_w_qkv, sb_w_o, sb_bias, rw_mu, rw_w_r, rw_w_k, rw_w_v, rw_w_o, rw_w0, rw_w1, rw_w2, rw_a0, rw_a1, rw_a2, rw_g1, rw_g2, rw_k_k, rw_k_a, rw_r_k, rw_ln_g, rw_ln_b, final_norm):
    batch, seq, _ = x_prompt.shape
    dbatch, dseq, _ = x_sample.shape
    depth = norm_mix.shape[0]
    n_p, n_s = batch * seq, dbatch * dseq
    tm_p, tm_s = 512, n_s
    tps_p, tps_s = seq // tm_p, 1

    mod_mix = _ada_mod(jnp.concatenate([c_prompt, c_sample], axis=0), ada_w_mix, ada_b_mix)
    mod_ffn = _ada_mod(jnp.concatenate([c_prompt, c_sample], axis=0), ada_w_ffn, ada_b_ffn)

    def split(mod):
        parts = [mod[:, n * D:(n + 1) * D] for n in range(3)]
        return ([m[:batch].reshape(batch, 1, D) for m in parts],
                [jnp.repeat(m[batch:], dseq, axis=0).reshape(1, n_s, D) for m in parts])

    xp = x_prompt.reshape(n_p, D)
    xs = x_sample.reshape(n_s, D)
    fin = final_norm.reshape(1, D)
    outs = {}

    for i in range(depth):
        j = i // 2
        g_mix = norm_mix[i].reshape(1, D)
        (sh_p, sc_p, ga_p), (sh_s, sc_s, ga_s) = split(mod_mix[i])
        if i % 2 == 0:
            w_qkv = sb_w_qkv[j].astype(BF16)
            w_o = sb_w_o[j].astype(BF16)
            qp, ktp, vtp, ktbp, _, vbp = _qkv(xp, g_mix, sc_p, sh_p, w_qkv, tm_p, tps_p, BF16)
            qs, kts, vts, _, ks, vs = _qkv(xs, g_mix, sc_s, sh_s, w_qkv, tm_s, tps_s, F32)
            op = _sb_prompt(qp, ktbp, vbp, sb_bias[j], batch, seq, 512)
            n_phys, page = cache_k.shape[1], cache_k.shape[2]
            cache_kt = jnp.transpose(cache_k[j], (0, 2, 3, 1)).reshape(n_phys, D, page)
            cache_vt = jnp.transpose(cache_v[j], (0, 2, 3, 1)).reshape(n_phys, D, page)
            os_ = _sb_sample(qs, kts[0], vts[0], cache_kt, cache_vt, page_table, sb_bias[j], dseq)
            xp = _proj_res(op, xp, ga_p, w_o, tm_p, tps_p)
            xs = _proj_res(os_, xs, ga_s, w_o, tm_s, tps_s)
            outs.setdefault("kp", []).append(
                jnp.transpose(ktp.reshape(batch, HEADS, HD, seq), (0, 3, 1, 2)))
            outs.setdefault("vp", []).append(
                jnp.transpose(vtp.reshape(batch, HEADS, HD, seq), (0, 3, 1, 2)))
            outs.setdefault("ks", []).append(ks.reshape(dbatch, dseq, HEADS, HD))
            outs.setdefault("vs", []).append(vs.reshape(dbatch, dseq, HEADS, HD))
        else:
            pad = GATE_LORA_PAD - rw_g1.shape[-1]
            mats = (jnp.pad(rw_mu[j], ((0, 2), (0, 0))),
                    rw_w_r[j].T.astype(BF16), rw_w_k[j].T.astype(BF16), rw_w_v[j].T.astype(BF16),
                    rw_w1[j].astype(BF16), rw_w2[j].T.astype(BF16),
                    rw_a1[j].astype(BF16), rw_a2[j].T.astype(BF16),
                    jnp.pad(rw_g1[j], ((0, 0), (0, pad))).astype(BF16),
                    jnp.pad(rw_g2[j], ((0, pad), (0, 0))).T.astype(BF16))
            cols = (_col(rw_w0[j]), _col(rw_a0[j]), _col(rw_k_k[j]), _col(rw_k_a[j]))
            ocols = (_col(rw_ln_g[j]), _col(rw_ln_b[j]), _col(rw_r_k[j]))
            w_o = rw_w_o[j].astype(BF16)

            hp = _norm(xp, g_mix, sc_p, sh_p, tm_p, tps_p).reshape(batch, seq, D)
            hs = _norm(xs, g_mix, sc_s, sh_s, tm_s, tps_s).reshape(dbatch, dseq, D)
            prev_p = jnp.concatenate([jnp.zeros((batch, 1, D), F32), hp[:, :-1]], axis=1)
            prev_s = jnp.concatenate([state_shift[j][:, None, :], hs[:, :-1]], axis=1)

            rp, _, lwp, kp_, kkp, bp, vp_, gp = _rw_proj(hp.reshape(n_p, D), prev_p.reshape(n_p, D),
                                                         mats, cols, 256)
            rs, ds, _, ks_, kks, bs, vs_, gs = _rw_proj(hs.reshape(n_s, D), prev_s.reshape(n_s, D),
                                                        mats, cols, n_s)
            yp, wkv_p = _rw_chunk_prompt(rp, lwp, kp_, kkp, bp, vp_, batch, seq)
            s0 = state_wkv[j].reshape(dbatch, PAIRS, LANES, HD)
            ys, wkv_s = _rw_scan_sample(rs, ds, ks_, kks, bs, vs_, s0, dbatch, dseq)
            xp = _rw_out(yp, rp, kp_, vp_, gp, xp, ga_p, w_o, ocols, 256, seq // 256)
            xs = _rw_out(ys, rs, ks_, vs_, gs, xs, ga_s, w_o, ocols, n_s, 1)
            outs.setdefault("wp", []).append(wkv_p.reshape(batch, HEADS, HD, HD))
            outs.setdefault("sp", []).append(hp[:, -1])
            outs.setdefault("ws", []).append(wkv_s.reshape(dbatch, HEADS, HD, HD))
            outs.setdefault("ss", []).append(hs[:, -1])

        g_ffn = norm_ffn[i].reshape(1, D)
        (sh_p, sc_p, ga_p), (sh_s, sc_s, ga_s) = split(mod_ffn[i])
        wu = w_up[i].astype(BF16)
        wd = w_down[i].astype(BF16)
        last = i == depth - 1
        xp = _mlp(xp, g_ffn, sc_p, sh_p, ga_p, wu, wd, fin, tm_p, tps_p, last)
        xs = _mlp(xs, g_ffn, sc_s, sh_s, ga_s, wu, wd, fin, tm_s, tps_s, last)

    return (xp.reshape(batch, seq, D), xs.reshape(dbatch, dseq, D),
            jnp.stack(outs["kp"]), jnp.stack(outs["vp"]), jnp.stack(outs["ks"]),
            jnp.stack(outs["vs"]), jnp.stack(outs["wp"]), jnp.stack(outs["sp"]),
            jnp.stack(outs["ws"]), jnp.stack(outs["ss"]))
```
